```python
import jax
import jax.numpy as jnp
from jax import lax
import numpy as np

D_MODEL = 2048
BATCH = 16
SEQ = 256
DEPTH = 2
DEC_BATCH = 8
DEC_SEQ = 4096
PAST_LEN = 256

GRID_W = 64
N_BRANCH = 4
MIX_W = D_MODEL // N_BRANCH
HEAD_DIM = 128
A_HEADS = MIX_W // HEAD_DIM
A_KV = A_HEADS // 2
A_GROUP = A_HEADS // A_KV
A_HD = HEAD_DIM
WINDOW = 128
Q_BLOCK = 128
M_HEADS = MIX_W // HEAD_DIM
M_DK = HEAD_DIM
M_DV = HEAD_DIM
R_HEADS = MIX_W // HEAD_DIM
R_DK = HEAD_DIM
R_DV = HEAD_DIM
G_HEADS = MIX_W // HEAD_DIM
G_DK = HEAD_DIM // 2
G_DV = HEAD_DIM
G_RANK = 16
GLA_TAU = 16.0
CHUNK = 64
FFN_HID = ((8 * D_MODEL // 3 + 255) // 256) * 256
ROPE_BASE = 10000.0
EPS = 1e-6

IN_SIZES = (
    A_HEADS * A_HD, A_KV * A_HD, A_KV * A_HD,
    M_HEADS * M_DK, M_HEADS * M_DK, M_HEADS * M_DV, M_HEADS * M_DV, 4 * M_HEADS,
    R_HEADS * R_DK, R_HEADS * R_DK, R_HEADS * R_DV, R_HEADS * R_DV,
    G_HEADS * G_DK, G_HEADS * G_DK, G_HEADS * G_DV, G_HEADS * G_DV, 2 * G_RANK,
)
IN_OFFSETS = tuple(int(s) for s in np.cumsum(IN_SIZES)[:-1])
D_IN = int(sum(IN_SIZES))

kernel_name = "hybrid_bidir_diffusion_step"

F32 = jnp.float32


def rms_norm(x, w):
    xf = x.astype(F32)
    y = xf * lax.rsqrt(jnp.mean(xf * xf, axis=-1, keepdims=True) + EPS)
    return (y * w.astype(F32)).astype(x.dtype)


def head_rms_norm(y, w):
    B, T, H, d = y.shape
    y = y * lax.rsqrt(jnp.mean(y * y, axis=-1, keepdims=True) + EPS)
    return y.reshape(B, T, H * d) * w.astype(F32)


def axial_rope(x):
    B, T, H, d = x.shape
    rows = T // GRID_W
    row = jnp.repeat(jnp.arange(rows), GRID_W).astype(F32)
    col = jnp.tile(jnp.arange(GRID_W), rows).astype(F32)
    quarter = d // 4
    inv = ROPE_BASE ** (-jnp.arange(quarter, dtype=F32) / quarter)

    def rot(xa, pos):
        ang = pos[:, None] * inv[None, :]
        cos = jnp.cos(ang)[None, :, None, :]
        sin = jnp.sin(ang)[None, :, None, :]
        x1, x2 = xa[..., :quarter], xa[..., quarter:]
        return jnp.concatenate([x1 * cos - x2 * sin, x2 * cos + x1 * sin], axis=-1)

    xf = x.astype(F32)
    out = jnp.concatenate([rot(xf[..., : d // 2], row), rot(xf[..., d // 2:], col)], axis=-1)
    return out.astype(x.dtype)


def to_chunks(x):
    B, T = x.shape[:2]
    x = x.reshape((B, T // CHUNK, CHUNK) + x.shape[2:])
    return jnp.moveaxis(x, (1, 2), (0, 3))


def from_chunks(y):
    y = jnp.moveaxis(y, (0, 3), (1, 2))
    B, nc, L = y.shape[:3]
    return y.reshape((B, nc * L) + y.shape[3:])


def sink_attend(s, v, sink):
    sk = sink.astype(F32)[None, :, :, None, None]
    m = jnp.maximum(jnp.max(s, axis=-1, keepdims=True), sk)
    p = jnp.exp(s - m)
    p = p / (jnp.sum(p, axis=-1, keepdims=True) + jnp.exp(sk - m))
    return jnp.einsum("bkgqs,bskd->bqkgd", p.astype(v.dtype), v)


def context_attention(q, k, v, sink):
    B, T, H, d = q.shape
    nb = T // Q_BLOCK
    scale = d ** -0.5
    qb = jnp.moveaxis(q.reshape(B, nb, Q_BLOCK, A_KV, A_GROUP, d), 1, 0)

    def block(qblk):
        s = jnp.einsum("bqkgd,bskd->bkgqs", qblk, k).astype(F32) * scale
        return sink_attend(s, v, sink)

    out = lax.map(block, qb)
    return jnp.moveaxis(out, 0, 1).reshape(B, T, H * d)


def latent_window_attention(q, k, v, k_ctx, v_ctx, sink):
    B, T, H, d = q.shape
    nb = T // Q_BLOCK
    scale = d ** -0.5
    qg = q.reshape(B, T, A_KV, A_GROUP, d)
    pad = ((0, 0), (Q_BLOCK, Q_BLOCK), (0, 0), (0, 0))
    kpad = jnp.pad(k, pad)
    vpad = jnp.pad(v, pad)
    qi = jnp.arange(Q_BLOCK)[:, None]
    kj = jnp.arange(3 * Q_BLOCK)[None, :]
    band = jnp.abs(qi + Q_BLOCK - kj) <= WINDOW

    def block(b):
        start = b * Q_BLOCK
        qb = lax.dynamic_slice_in_dim(qg, start, Q_BLOCK, axis=1)
        kb = lax.dynamic_slice_in_dim(kpad, start, 3 * Q_BLOCK, axis=1)
        vb = lax.dynamic_slice_in_dim(vpad, start, 3 * Q_BLOCK, axis=1)
        kpos = start - Q_BLOCK + jnp.arange(3 * Q_BLOCK)
        valid = band & ((kpos >= 0) & (kpos < T))[None, :]
        s_loc = jnp.einsum("bqkgd,bskd->bkgqs", qb, kb).astype(F32) * scale
        s_loc = jnp.where(valid, s_loc, -jnp.inf)
        s_ctx = jnp.einsum("bqkgd,bskd->bkgqs", qb, k_ctx.astype(qb.dtype)).astype(F32) * scale
        s = jnp.concatenate([s_loc, s_ctx], axis=-1)
        v_all = jnp.concatenate([vb, v_ctx.astype(vb.dtype)], axis=1)
        return sink_attend(s, v_all, sink)

    out = lax.map(block, jnp.arange(nb))
    return jnp.moveaxis(out, 0, 1).reshape(B, T, H * d)


def mlstm_scan(q, k, v, ig, lf, C0, n0, m0):
    dk = q.shape[-1]
    causal = jnp.tril(jnp.ones((CHUNK, CHUNK), dtype=bool))

    def step(carry, inp):
        C, n, m = carry
        qc, kc, vc, ic, fc = inp
        b = jnp.cumsum(fc, axis=-1)
        a = b + m[..., None]
        D = jnp.where(causal, b[..., :, None] - b[..., None, :] + ic[..., None, :], -jnp.inf)
        mt = jnp.maximum(a, jnp.max(D, axis=-1))
        w_state = jnp.exp(a - mt)
        s = jnp.einsum("bhtd,bhsd->bhts", qc, kc) * jnp.exp(D - mt[..., None])
        num = w_state[..., None] * jnp.einsum("bhtd,bhde->bhte", qc, C) + jnp.einsum("bhts,bhse->bhte", s, vc)
        den = w_state * jnp.einsum("bhtd,bhd->bht", qc, n) + jnp.sum(s, axis=-1)
        h = num / jnp.maximum(jnp.abs(den), jnp.exp(-mt))[..., None]
        bl = b[..., -1]
        g = bl[..., None] - b + ic
        m_new = jnp.maximum(bl + m, jnp.max(g, axis=-1))
        w_old = jnp.exp(bl + m - m_new)
        w_new = jnp.exp(g - m_new[..., None])
        C = w_old[..., None, None] * C + jnp.einsum("bhs,bhsd,bhse->bhde", w_new, kc, vc)
        n = w_old[..., None] * n + jnp.einsum("bhs,bhsd->bhd", w_new, kc)
        return (C, n, m_new), h

    xs = (to_chunks(q.astype(F32)), to_chunks(k.astype(F32)) * dk ** -0.5,
          to_chunks(v.astype(F32)), to_chunks(ig), to_chunks(lf))
    (C, n, m), h = lax.scan(step, (C0.astype(F32), n0.astype(F32), m0.astype(F32)), xs)
    return from_chunks(h), (C, n, m)


def retention_scan(q, k, v, S0, log_gamma):
    dk = q.shape[-1]
    idx = jnp.arange(CHUNK, dtype=F32)
    causal = idx[:, None] >= idx[None, :]
    lg = log_gamma.astype(F32)
    decay = jnp.exp(jnp.where(causal, (idx[:, None] - idx[None, :]) * lg[:, None, None], -jnp.inf))
    xi = jnp.exp((idx + 1.0)[None, :] * lg[:, None])
    zeta = jnp.exp((CHUNK - 1.0 - idx)[None, :] * lg[:, None])
    g_chunk = jnp.exp(CHUNK * lg)

    def step(S, inp):
        qc, kc, vc = inp
        att = jnp.einsum("bhtd,bhsd->bhts", qc, kc) * decay
        y = jnp.einsum("bhts,bhse->bhte", att, vc) + xi[:, :, None] * jnp.einsum("bhtd,bhde->bhte", qc, S)
        S = g_chunk[:, None, None] * S + jnp.einsum("bhsd,bhse->bhde", kc * zeta[:, :, None], vc)
        return S, y

    xs = (to_chunks(q.astype(F32)), to_chunks(k.astype(F32)) * dk ** -0.5, to_chunks(v.astype(F32)))
    S, y = lax.scan(step, S0.astype(F32), xs)
    return from_chunks(y), S


def gla_scan(q, k, v, log_alpha, S0):
    dk = q.shape[-1]
    causal = jnp.tril(jnp.ones((CHUNK, CHUNK), dtype=bool))[None, None, :, :, None]

    def step(S, inp):
        qc, kc, vc, ac = inp
        cum = jnp.cumsum(ac, axis=2)
        rel = jnp.exp(jnp.where(causal, cum[:, :, :, None, :] - cum[:, :, None, :, :], -jnp.inf))
        att = jnp.einsum("bhtd,bhsd,bhtsd->bhts", qc, kc, rel)
        y = jnp.einsum("bhts,bhse->bhte", att, vc) + jnp.einsum("bhtd,bhde->bhte", qc * jnp.exp(cum), S)
        last = cum[:, :, -1]
        S = jnp.exp(last)[..., None] * S + jnp.einsum("bhsd,bhse->bhde", kc * jnp.exp(last[:, :, None, :] - cum), vc)
        return S, y

    xs = (to_chunks(q.astype(F32)) * dk ** -0.5, to_chunks(k.astype(F32)),
          to_chunks(v.astype(F32)), to_chunks(log_alpha))
    S, y = lax.scan(step, S0.astype(F32), xs)
    return from_chunks(y), S


def run_direction(scan_fn, seqs, init, reverse, *extra):
    if reverse:
        seqs = tuple(jnp.flip(a, axis=1) for a in seqs)
    y, state = scan_fn(*seqs, *init, *extra)
    if reverse:
        y = jnp.flip(y, axis=1)
    return y, state


def token_mixers(h, p, l, cache):
    B, T, _ = h.shape
    is_ctx = cache is None
    z = jnp.einsum("btd,de->bte", h, p["w_in"][l])
    (aq, ak, av, mq, mk, mv, mo, mg, rq, rk, rv, rg,
     gq, gk, gv, gg, glr) = jnp.split(z, IN_OFFSETS, axis=-1)

    def heads(a, n):
        return a.reshape(B, T, n, -1)

    q_a, k_a, v_a = heads(aq, A_HEADS), heads(ak, A_KV), heads(av, A_KV)
    sink = p["attn_sink"][l].reshape(A_KV, A_GROUP)
    if is_ctx:
        y_a = context_attention(q_a, k_a, v_a, sink)
    else:
        y_a = latent_window_attention(axial_rope(q_a), axial_rope(k_a), v_a, cache[0], cache[1], sink)

    q, k, v = heads(mq, M_HEADS), heads(mk, M_HEADS), heads(mv, M_HEADS)
    gates = mg.astype(F32).reshape(B, T, 2, 2, M_HEADS) + p["mlstm_if_b"][l].astype(F32)
    h_m, st_m = [], []
    for d in range(2):
        if is_ctx:
            init = (jnp.zeros((B, M_HEADS, M_DK, M_DV), F32), jnp.zeros((B, M_HEADS, M_DK), F32),
                    jnp.zeros((B, M_HEADS), F32))
        else:
            init = (cache[2][:, d], cache[3][:, d], cache[4][:, d])
        seqs = (q, k, v, gates[:, :, d, 0], jax.nn.log_sigmoid(gates[:, :, d, 1]))
        y_d, s_d = run_direction(mlstm_scan, seqs, init, d == 1)
        h_m.append(y_d)
        st_m.append(s_d)
    y_m = jax.nn.sigmoid(mo.astype(F32)) * head_rms_norm(h_m[0] + h_m[1], p["mlstm_norm_w"][l])

    q, k, v = heads(rq, R_HEADS), heads(rk, R_HEADS), heads(rv, R_HEADS)
    h_r, st_r = [], []
    for d in range(2):
        log_gamma = jax.nn.log_sigmoid(p["ret_decay"][l, d].astype(F32))
        init = (jnp.zeros((B, R_HEADS, R_DK, R_DV), F32),) if is_ctx else (cache[5][:, d],)
        y_d, s_d = run_direction(retention_scan, (q, k, v), init, d == 1, log_gamma)
        h_r.append(y_d)
        st_r.append(s_d)
    y_r = jax.nn.silu(rg.astype(F32)) * head_rms_norm(h_r[0] + h_r[1], p["ret_norm_w"][l])

    q, k, v = heads(gq, G_HEADS), heads(gk, G_HEADS), heads(gv, G_HEADS)
    lr = glr.reshape(B, T, 2, G_RANK)
    h_g, st_g = [], []
    for d in range(2):
        logit = jnp.einsum("btr,re->bte", lr[:, :, d], p["gla_w2"][l, d]) + p["gla_b"][l, d]
        la = (jax.nn.log_sigmoid(logit.astype(F32)) / GLA_TAU).reshape(B, T, G_HEADS, G_DK)
        init = (jnp.zeros((B, G_HEADS, G_DK, G_DV), F32),) if is_ctx else (cache[6][:, d],)
        y_d, s_d = run_direction(gla_scan, (q, k, v, la), init, d == 1)
        h_g.append(y_d)
        st_g.append(s_d)
    y_g = jax.nn.silu(gg.astype(F32)) * head_rms_norm(h_g[0] + h_g[1], p["gla_norm_w"][l])

    ys = (y_a, y_m.astype(h.dtype), y_r.astype(h.dtype), y_g.astype(h.dtype))
    merged = None
    for b in range(N_BRANCH):
        gate = jax.nn.sigmoid(jnp.einsum("btd,de->bte", h, p["w_mgate"][l, b]).astype(F32)).astype(h.dtype)
        term = gate * jnp.einsum("btc,cd->btd", ys[b], p["w_br"][l, b])
        merged = term if merged is None else merged + term
    out = jnp.einsum("btd,de->bte", merged, p["w_out"][l])

    if is_ctx:
        states = (k_a, v_a,
                  jnp.stack([st_m[0][0], st_m[1][0]], axis=1),
                  jnp.stack([st_m[0][1], st_m[1][1]], axis=1),
                  jnp.stack([st_m[0][2], st_m[1][2]], axis=1),
                  jnp.stack(st_r, axis=1),
                  jnp.stack(st_g, axis=1))
    else:
        states = None
    return out, states


def layer(x, cond, p, l, cache):
    mod = jnp.einsum("...d,de->...e", jax.nn.silu(cond), p["w_ada"][l]) + p["b_ada"][l]
    sh1, sc1, g1, sh2, sc2, g2 = jnp.split(mod, 6, axis=-1)
    h = rms_norm(x, p["norm1_w"][l]) * (1 + sc1) + sh1
    mix, states = token_mixers(h, p, l, cache)
    x = x + g1 * mix
    h = rms_norm(x, p["norm2_w"][l]) * (1 + sc2) + sh2
    gt, up = jnp.split(jnp.einsum("btd,df->btf", h, p["ffn_w_gu"][l]), 2, axis=-1)
    x = x + g2 * jnp.einsum("btf,fd->btd", jax.nn.silu(gt) * up, p["ffn_w_down"][l])
    return x, states


def setup_inputs(seed: int = 0) -> dict:
    key = jax.random.key(seed)
    ks = jax.random.split(key, 32)
    D = D_MODEL

    def nrm(k, shape, s):
        return jax.random.normal(k, shape, F32) * s

    return {
        "x_prompt": nrm(ks[0], (BATCH, SEQ, D), 1.0),
        "x_sample": nrm(ks[1], (DEC_BATCH, DEC_SEQ, D), 1.0),
        "cache_attn_k": nrm(ks[2], (DEC_BATCH, DEPTH, PAST_LEN, A_KV, A_HD), 1.0),
        "cache_attn_v": nrm(ks[3], (DEC_BATCH, DEPTH, PAST_LEN, A_KV, A_HD), 1.0),
        "state_mlstm_C": nrm(ks[4], (DEC_BATCH, DEPTH, 2, M_HEADS, M_DK, M_DV), 0.1),
        "state_mlstm_n": nrm(ks[5], (DEC_BATCH, DEPTH, 2, M_HEADS, M_DK), 0.1),
        "state_mlstm_m": nrm(ks[6], (DEC_BATCH, DEPTH, 2, M_HEADS), 1.0),
        "state_ret_S": nrm(ks[7], (DEC_BATCH, DEPTH, 2, R_HEADS, R_DK, R_DV), 0.5),
        "state_gla_S": nrm(ks[8], (DEC_BATCH, DEPTH, 2, G_HEADS, G_DK, G_DV), 0.5),
        "c": nrm(ks[9], (DEC_BATCH, D), 1.0),
        "c_ctx": nrm(ks[10], (D,), 1.0),
        "w_ada": nrm(ks[11], (DEPTH, D, 6 * D), 0.5 * D ** -0.5),
        "b_ada": nrm(ks[12], (DEPTH, 6 * D), 0.02),
        "norm1_w": 1.0 + nrm(ks[13], (DEPTH, D), 0.02),
        "norm2_w": 1.0 + nrm(ks[14], (DEPTH, D), 0.02),
        "w_in": nrm(ks[15], (DEPTH, D, D_IN), D ** -0.5),
        "attn_sink": nrm(ks[16], (DEPTH, A_HEADS), 1.0),
        "mlstm_if_b": nrm(ks[17], (DEPTH, 2, 2, M_HEADS), 0.1) + jnp.array([0.0, 3.0], F32)[None, None, :, None],
        "mlstm_norm_w": 1.0 + nrm(ks[18], (DEPTH, M_HEADS * M_DV), 0.02),
        "ret_decay": jnp.log(2.0 ** (5.0 + jnp.arange(R_HEADS, dtype=F32)) - 1.0) + nrm(ks[19], (DEPTH, 2, R_HEADS), 0.05),
        "ret_norm_w": 1.0 + nrm(ks[20], (DEPTH, R_HEADS * R_DV), 0.02),
        "gla_w2": nrm(ks[21], (DEPTH, 2, G_RANK, G_HEADS * G_DK), G_RANK ** -0.5),
        "gla_b": nrm(ks[22], (DEPTH, 2, G_HEADS * G_DK), 0.1),
        "gla_norm_w": 1.0 + nrm(ks[23], (DEPTH, G_HEADS * G_DV), 0.02),
        "w_br": nrm(ks[24], (DEPTH, N_BRANCH, MIX_W, D), MIX_W ** -0.5),
        "w_mgate": nrm(ks[25], (DEPTH, N_BRANCH, D, D), D ** -0.5),
        "w_out": nrm(ks[26], (DEPTH, D, D), D ** -0.5),
        "ffn_w_gu": nrm(ks[27], (DEPTH, D, 2 * FFN_HID), D ** -0.5),
        "ffn_w_down": nrm(ks[28], (DEPTH, FFN_HID, D), FFN_HID ** -0.5),
        "final_norm_w": 1.0 + nrm(ks[29], (D,), 0.02),
    }


def reference(x_prompt, x_sample, cache_attn_k, cache_attn_v, state_mlstm_C, state_mlstm_n,
              state_mlstm_m, state_ret_S, state_gla_S, c, c_ctx, w_ada, b_ada, norm1_w, norm2_w,
              w_in, attn_sink, mlstm_if_b, mlstm_norm_w, ret_decay, ret_norm_w, gla_w2, gla_b,
              gla_norm_w, w_br, w_mgate, w_out, ffn_w_gu, ffn_w_down, final_norm_w):
    p = {"w_ada": w_ada, "b_ada": b_ada, "norm1_w": norm1_w, "norm2_w": norm2_w, "w_in": w_in,
         "attn_sink": attn_sink, "mlstm_if_b": mlstm_if_b, "mlstm_norm_w": mlstm_norm_w,
         "ret_decay": ret_decay, "ret_norm_w": ret_norm_w, "gla_w2": gla_w2, "gla_b": gla_b,
         "gla_norm_w": gla_norm_w, "w_br": w_br, "w_mgate": w_mgate, "w_out": w_out,
         "ffn_w_gu": ffn_w_gu, "ffn_w_down": ffn_w_down}

    xp = x_prompt
    ctx_states = []
    for l in range(DEPTH):
        xp, st = layer(xp, c_ctx[None, None, :], p, l, None)
        ctx_states.append(st)
    y_prompt = rms_norm(xp, final_norm_w)
    new_attn_k = jnp.stack([s[0] for s in ctx_states], axis=1)
    new_attn_v = jnp.stack([s[1] for s in ctx_states], axis=1)
    new_mlstm_C = jnp.stack([s[2] for s in ctx_states], axis=1)
    new_mlstm_n = jnp.stack([s[3] for s in ctx_states], axis=1)
    new_mlstm_m = jnp.stack([s[4] for s in ctx_states], axis=1)
    new_ret_S = jnp.stack([s[5] for s in ctx_states], axis=1)
    new_gla_S = jnp.stack([s[6] for s in ctx_states], axis=1)

    xs = x_sample
    for l in range(DEPTH):
        cache = (cache_attn_k[:, l], cache_attn_v[:, l], state_mlstm_C[:, l], state_mlstm_n[:, l],
                 state_mlstm_m[:, l], state_ret_S[:, l], state_gla_S[:, l])
        xs, _ = layer(xs, c[:, None, :], p, l, cache)
    y_sample = rms_norm(xs, final_norm_w)

    return (y_prompt, y_sample, new_attn_k, new_attn_v, new_mlstm_C, new_mlstm_n, new_mlstm_m, new_ret_S, new_gla_S)
```

```python
import functools

import numpy as np
import jax
import jax.numpy as jnp
from jax import lax
from jax.experimental import pallas as pl
from jax.experimental.pallas import tpu as pltpu

F32 = jnp.float32
BF16 = jnp.bfloat16

HEAD_DIM = 128
N_HEADS = 4
A_KV = 2
GRID_W = 64
WINDOW = 128
Q_BLOCK = 128
G_DK = 64
G_RANK = 16
GLA_TAU = 16.0
ROPE_BASE = 10000.0
EPS = 1e-6
MOD_ROWS = 16
SCAN_CHUNK = 128
GLA_CHUNK = 64
MIX_W = N_HEADS * HEAD_DIM

Z_AQ, Z_AK, Z_AV = 0, 512, 768
Z_MQ, Z_MK, Z_MV, Z_MO = 1024, 1536, 2048, 2560
Z_RQ, Z_RK, Z_RV, Z_RG = 3072, 3584, 4096, 4608
Z_GQ, Z_GK, Z_GV, Z_GG = 5120, 5376, 5632, 6144
Z_TAIL = 6656
Z_COLS = 6912
Z_TN = 768
TAIL_MG, TAIL_GLR = 0, 16

VMEM_LIMIT = 56 * 2 ** 20


def _cparams(*sem):
    return pltpu.CompilerParams(dimension_semantics=sem, vmem_limit_bytes=VMEM_LIMIT)


def _log_sigmoid(x):
    return jnp.minimum(x, 0.0) - jnp.log1p(jnp.exp(-jnp.abs(x)))


def _split3(x):
    hi = x.astype(BF16)
    r = x - hi.astype(F32)
    mid = r.astype(BF16)
    lo = (r - mid.astype(F32)).astype(BF16)
    return hi, mid, lo


def _dot(a, b):
    return jnp.dot(a, b, preferred_element_type=F32)


def _dot_nt(a, b):
    return lax.dot_general(a, b, (((1,), (1,)), ((), ())), preferred_element_type=F32)


def _ada_kernel(cond_ref, w_ref, b_ref, o_ref):
    s = jax.nn.silu(cond_ref[...]).astype(BF16)
    o_ref[0] = _dot(s, w_ref[0].astype(BF16)) + b_ref[0]


def ada_modulation(cond, w_ada, b_ada):
    depth, d, n = w_ada.shape
    tn = 1024
    return pl.pallas_call(
        _ada_kernel,
        grid=(depth, n // tn),
        in_specs=[pl.BlockSpec((MOD_ROWS, d), lambda l, j: (0, 0)),
                  pl.BlockSpec((1, d, tn), lambda l, j: (l, 0, j)),
                  pl.BlockSpec((1, 1, tn), lambda l, j: (l, 0, j))],
        out_specs=pl.BlockSpec((1, MOD_ROWS, tn), lambda l, j: (l, 0, j)),
        out_shape=jax.ShapeDtypeStruct((depth, MOD_ROWS, n), F32),
        compiler_params=_cparams("arbitrary", "arbitrary"),
        name="ada_modulation",
    )(cond, w_ada, b_ada.reshape(depth, 1, n))


def _modulated_norm(x, nw, sh, sc):
    y = x * lax.rsqrt(jnp.mean(x * x, axis=-1, keepdims=True) + EPS)
    return (y * nw) * (1.0 + sc) + sh


def _in_proj_kernel(x_ref, sh_ref, sc_ref, nw_ref, w_ref, z_ref, h_ref, *, tm, group_rows):
    @pl.when(pl.program_id(1) == 0)
    def _():
        g = (pl.program_id(0) * tm) // group_rows
        h = _modulated_norm(x_ref[...], nw_ref[...], sh_ref[pl.ds(g, 1), :], sc_ref[pl.ds(g, 1), :])
        h_ref[...] = h.astype(BF16)

    z_ref[...] = _dot(h_ref[...], w_ref[...])


def in_projection(x, sh, sc, nw, w, *, group_rows, tm):
    rows, d = x.shape
    n = w.shape[1]
    return pl.pallas_call(
        functools.partial(_in_proj_kernel, tm=tm, group_rows=group_rows),
        grid=(rows // tm, n // Z_TN),
        in_specs=[pl.BlockSpec((tm, d), lambda i, j: (i, 0)),
                  pl.BlockSpec((MOD_ROWS, d), lambda i, j: (0, 0)),
                  pl.BlockSpec((MOD_ROWS, d), lambda i, j: (0, 0)),
                  pl.BlockSpec((1, d), lambda i, j: (0, 0)),
                  pl.BlockSpec((d, Z_TN), lambda i, j: (0, j))],
        out_specs=[pl.BlockSpec((tm, Z_TN), lambda i, j: (i, j)),
                   pl.BlockSpec((tm, d), lambda i, j: (i, 0))],
        out_shape=[jax.ShapeDtypeStruct((rows, n), F32), jax.ShapeDtypeStruct((rows, d), BF16)],
        compiler_params=_cparams("arbitrary", "arbitrary"),
        name="in_projection",
    )(x, sh, sc, nw, w)


def _merge_kernel(h_ref, ya_ref, ym_ref, yr_ref, yg_ref, wg_ref, wb_ref, o_ref):
    h = h_ref[...]
    acc = None
    for b, y_ref in enumerate((ya_ref, ym_ref, yr_ref, yg_ref)):
        gate = jax.nn.sigmoid(_dot(h, wg_ref[b]))
        term = gate * _dot(y_ref[...], wb_ref[b])
        acc = term if acc is None else acc + term
    o_ref[...] = acc.astype(BF16)


def merge_branches(h, ys, w_mgate, w_br, *, tm, tn):
    rows, d = h.shape
    nb, mw, _ = w_br.shape
    return pl.pallas_call(
        _merge_kernel,
        grid=(rows // tm, d // tn),
        in_specs=[pl.BlockSpec((tm, d), lambda i, j: (i, 0))]
        + [pl.BlockSpec((tm, mw), lambda i, j: (i, 0))] * nb
        + [pl.BlockSpec((nb, d, tn), lambda i, j: (0, 0, j)),
           pl.BlockSpec((nb, mw, tn), lambda i, j: (0, 0, j))],
        out_specs=pl.BlockSpec((tm, tn), lambda i, j: (i, j)),
        out_shape=jax.ShapeDtypeStruct((rows, d), BF16),
        compiler_params=_cparams("arbitrary", "arbitrary"),
        name="merge_branches",
    )(h, *ys, w_mgate, w_br)


def _residual_proj_kernel(a_ref, w_ref, x_ref, g_ref, o_ref, *, tm, group_rows):
    g = (pl.program_id(0) * tm) // group_rows
    o_ref[...] = x_ref[...] + g_ref[pl.ds(g, 1), :] * _dot(a_ref[...], w_ref[...])


def residual_projection(a, w, x, gate, *, group_rows, tm, tn):
    rows, k = a.shape
    n = w.shape[1]
    return pl.pallas_call(
        functools.partial(_residual_proj_kernel, tm=tm, group_rows=group_rows),
        grid=(rows // tm, n // tn),
        in_specs=[pl.BlockSpec((tm, k), lambda i, j: (i, 0)),
                  pl.BlockSpec((k, tn), lambda i, j: (0, j)),
                  pl.BlockSpec((tm, tn), lambda i, j: (i, j)),
                  pl.BlockSpec((MOD_ROWS, tn), lambda i, j: (0, j))],
        out_specs=pl.BlockSpec((tm, tn), lambda i, j: (i, j)),
        out_shape=jax.ShapeDtypeStruct((rows, n), F32),
        compiler_params=_cparams("arbitrary", "arbitrary"),
        name="residual_projection",
    )(a, w, x, gate)


def _ffn_up_kernel(x_ref, sh_ref, sc_ref, nw_ref, wg_ref, wu_ref, o_ref, h_ref, *, tm, group_rows):
    @pl.when(pl.program_id(1) == 0)
    def _():
        g = (pl.program_id(0) * tm) // group_rows
        h = _modulated_norm(x_ref[...], nw_ref[...], sh_ref[pl.ds(g, 1), :], sc_ref[pl.ds(g, 1), :])
        h_ref[...] = h.astype(BF16)

    h = h_ref[...]
    o_ref[...] = (jax.nn.silu(_dot(h, wg_ref[...])) * _dot(h, wu_ref[...])).astype(BF16)


def ffn_up(x, sh, sc, nw, w_gu, *, group_rows, tm, tn):
    rows, d = x.shape
    hid = w_gu.shape[1] // 2
    nj = hid // tn
    return pl.pallas_call(
        functools.partial(_ffn_up_kernel, tm=tm, group_rows=group_rows),
        grid=(rows // tm, nj),
        in_specs=[pl.BlockSpec((tm, d), lambda i, j: (i, 0)),
                  pl.BlockSpec((MOD_ROWS, d), lambda i, j: (0, 0)),
                  pl.BlockSpec((MOD_ROWS, d), lambda i, j: (0, 0)),
                  pl.BlockSpec((1, d), lambda i, j: (0, 0)),
                  pl.BlockSpec((d, tn), lambda i, j: (0, j)),
                  pl.BlockSpec((d, tn), lambda i, j: (0, j + nj))],
        out_specs=pl.BlockSpec((tm, tn), lambda i, j: (i, j)),
        out_shape=jax.ShapeDtypeStruct((rows, hid), BF16),
        scratch_shapes=[pltpu.VMEM((tm, d), BF16)],
        compiler_params=_cparams("arbitrary", "arbitrary"),
        name="ffn_up",
    )(x, sh, sc, nw, w_gu, w_gu)


def _final_norm_kernel(x_ref, w_ref, o_ref):
    x = x_ref[...]
    o_ref[...] = x * lax.rsqrt(jnp.mean(x * x, axis=-1, keepdims=True) + EPS) * w_ref[...]


def final_norm(x, w, *, tm):
    rows, d = x.shape
    return pl.pallas_call(
        _final_norm_kernel,
        grid=(rows // tm,),
        in_specs=[pl.BlockSpec((tm, d), lambda i: (i, 0)), pl.BlockSpec((1, d), lambda i: (0, 0))],
        out_specs=pl.BlockSpec((tm, d), lambda i: (i, 0)),
        out_shape=jax.ShapeDtypeStruct((rows, d), F32),
        compiler_params=_cparams("arbitrary"),
        name="final_norm",
    )(x, w)


def _sink_softmax(s, sink_col):
    m = jnp.maximum(jnp.max(s, axis=-1, keepdims=True), sink_col)
    p = jnp.exp(s - m)
    return p / (jnp.sum(p, axis=-1, keepdims=True) + jnp.exp(sink_col - m))


def _sink_column(sink_ref, kv, rows):
    half = lax.broadcasted_iota(jnp.int32, (2 * rows, 1), 0) < rows
    return jnp.where(half, sink_ref[:, 2 * kv:2 * kv + 1], sink_ref[:, 2 * kv + 1:2 * kv + 2])


def _ctx_attn_kernel(q_ref, k_ref, v_ref, sink_ref, o_ref):
    t = q_ref.shape[0]
    scale = HEAD_DIM ** -0.5
    for kv in range(A_KV):
        lo = 2 * kv * HEAD_DIM
        q2 = jnp.concatenate([q_ref[:, lo:lo + HEAD_DIM], q_ref[:, lo + HEAD_DIM:lo + 2 * HEAD_DIM]], axis=0)
        k = k_ref[:, kv * HEAD_DIM:(kv + 1) * HEAD_DIM].astype(BF16)
        v = v_ref[:, kv * HEAD_DIM:(kv + 1) * HEAD_DIM].astype(BF16)
        s = _dot_nt(q2.astype(BF16), k) * scale
        p = _sink_softmax(s, _sink_column(sink_ref, kv, t))
        o = _dot(p.astype(BF16), v)
        o_ref[:, lo:lo + HEAD_DIM] = o[:t].astype(BF16)
        o_ref[:, lo + HEAD_DIM:lo + 2 * HEAD_DIM] = o[t:].astype(BF16)


def context_attention(z, sink_row, *, batch, seq):
    return pl.pallas_call(
        _ctx_attn_kernel,
        grid=(batch,),
        in_specs=[pl.BlockSpec((seq, MIX_W), lambda b: (b, Z_AQ // MIX_W)),
                  pl.BlockSpec((seq, 256), lambda b: (b, Z_AK // 256)),
                  pl.BlockSpec((seq, 256), lambda b: (b, Z_AV // 256)),
                  pl.BlockSpec((1, 128), lambda b: (0, 0))],
        out_specs=pl.BlockSpec((seq, MIX_W), lambda b: (b, 0)),
        out_shape=jax.ShapeDtypeStruct((batch * seq, MIX_W), BF16),
        compiler_params=_cparams("arbitrary"),
        name="context_attention",
    )(z, z, z, sink_row)


def _rope(x, cos, sin_signed):
    lane = lax.broadcasted_iota(jnp.int32, x.shape, 1)
    swapped = jnp.where(lane % 64 < 32, pltpu.roll(x, 96, 1), pltpu.roll(x, 32, 1))
    return x * cos + swapped * sin_signed


def _rope_kernel(q_ref, k_ref, cos_ref, sin_ref, qo_ref, ko_ref):
    cos, sin = cos_ref[...], sin_ref[...]
    for h in range(N_HEADS):
        sl = slice(h * HEAD_DIM, (h + 1) * HEAD_DIM)
        qo_ref[:, sl] = _rope(q_ref[:, sl], cos, sin).astype(BF16)
    for h in range(A_KV):
        sl = slice(h * HEAD_DIM, (h + 1) * HEAD_DIM)
        ko_ref[:, sl] = _rope(k_ref[:, sl], cos, sin).astype(BF16)


def rope_tables(seq):
    pos = np.arange(seq)
    quarter = HEAD_DIM // 4
    inv = jnp.asarray(ROPE_BASE, F32) ** (-jnp.arange(quarter, dtype=F32) / quarter)
    row = jnp.asarray(pos // GRID_W, F32)[:, None] * inv[None, :]
    col = jnp.asarray(pos % GRID_W, F32)[:, None] * inv[None, :]
    cos = jnp.concatenate([jnp.cos(row), jnp.cos(row), jnp.cos(col), jnp.cos(col)], axis=1)
    sin = jnp.concatenate([-jnp.sin(row), jnp.sin(row), -jnp.sin(col), jnp.sin(col)], axis=1)
    return cos, sin


def rope_qk(z, cos, sin, *, row0, batch, seq, tb):
    nt = seq // tb
    r0 = row0 // tb
    return pl.pallas_call(
        _rope_kernel,
        grid=(batch, nt),
        in_specs=[pl.BlockSpec((tb, MIX_W), lambda b, t: (r0 + b * nt + t, Z_AQ // MIX_W)),
                  pl.BlockSpec((tb, 256), lambda b, t: (r0 + b * nt + t, Z_AK // 256)),
                  pl.BlockSpec((tb, HEAD_DIM), lambda b, t: (t, 0)),
                  pl.BlockSpec((tb, HEAD_DIM), lambda b, t: (t, 0))],
        out_specs=[pl.BlockSpec((tb, MIX_W), lambda b, t: (b * nt + t, 0)),
                   pl.BlockSpec((tb, 256), lambda b, t: (b * nt + t, 0))],
        out_shape=[jax.ShapeDtypeStruct((batch * seq, MIX_W), BF16),
                   jax.ShapeDtypeStruct((batch * seq, 256), BF16)],
        compiler_params=_cparams("arbitrary", "arbitrary"),
        name="rope_qk",
    )(z, z, cos, sin)


def _lat_attn_kernel(q_ref, kp_ref, kc_ref, kn_ref, vp_ref, vc_ref, vn_ref, kx_ref, vx_ref, sink_ref,
                     o_ref, *, seq):
    qb = pl.program_id(1)
    nq = Q_BLOCK
    past = kx_ref.shape[2]
    scale = HEAD_DIM ** -0.5
    ncol = 3 * nq + past
    jj = lax.broadcasted_iota(jnp.int32, (2 * nq, ncol), 1)
    tt = lax.broadcasted_iota(jnp.int32, (2 * nq, ncol), 0) % nq
    kpos = qb * nq - nq + jj
    band = (jnp.abs(tt + nq - jj) <= WINDOW) & (kpos >= 0) & (kpos < seq)
    ok = (jj >= 3 * nq) | band
    for kv in range(A_KV):
        lo = 2 * kv * HEAD_DIM
        ks = slice(kv * HEAD_DIM, (kv + 1) * HEAD_DIM)
        q2 = jnp.concatenate([q_ref[:, lo:lo + HEAD_DIM], q_ref[:, lo + HEAD_DIM:lo + 2 * HEAD_DIM]], axis=0)
        kcat = jnp.concatenate([kp_ref[:, ks], kc_ref[:, ks], kn_ref[:, ks],
                                kx_ref[0, 0, :, ks].astype(BF16)], axis=0)
        vcat = jnp.concatenate([vp_ref[:, ks].astype(BF16), vc_ref[:, ks].astype(BF16),
                                vn_ref[:, ks].astype(BF16), vx_ref[0, 0, :, ks].astype(BF16)], axis=0)
        s = jnp.where(ok, _dot_nt(q2, kcat) * scale, -jnp.inf)
        p = _sink_softmax(s, _sink_column(sink_ref, kv, nq))
        o = _dot(p.astype(BF16), vcat)
        o_ref[:, lo:lo + HEAD_DIM] = o[:nq].astype(BF16)
        o_ref[:, lo + HEAD_DIM:lo + 2 * HEAD_DIM] = o[nq:].astype(BF16)


def latent_attention(qr, kr, z, cache_k, cache_v, sink_row, *, layer, row0, batch, seq):
    nb = seq // Q_BLOCK
    r0 = row0 // Q_BLOCK
    past = cache_k.shape[2]

    def prev(b, q):
        return b * nb + jnp.maximum(q - 1, 0)

    def nxt(b, q):
        return b * nb + jnp.minimum(q + 1, nb - 1)

    vcol = Z_AV // 256
    return pl.pallas_call(
        functools.partial(_lat_attn_kernel, seq=seq),
        grid=(batch, nb),
        in_specs=[pl.BlockSpec((Q_BLOCK, MIX_W), lambda b, q: (b * nb + q, 0)),
                  pl.BlockSpec((Q_BLOCK, 256), lambda b, q: (prev(b, q), 0)),
                  pl.BlockSpec((Q_BLOCK, 256), lambda b, q: (b * nb + q, 0)),
                  pl.BlockSpec((Q_BLOCK, 256), lambda b, q: (nxt(b, q), 0)),
                  pl.BlockSpec((Q_BLOCK, 256), lambda b, q: (r0 + prev(b, q), vcol)),
                  pl.BlockSpec((Q_BLOCK, 256), lambda b, q: (r0 + b * nb + q, vcol)),
                  pl.BlockSpec((Q_BLOCK, 256), lambda b, q: (r0 + nxt(b, q), vcol)),
                  pl.BlockSpec((1, 1, past, 256), lambda b, q: (b, layer, 0, 0)),
                  pl.BlockSpec((1, 1, past, 256), lambda b, q: (b, layer, 0, 0)),
                  pl.BlockSpec((1, 128), lambda b, q: (0, 0))],
        out_specs=pl.BlockSpec((Q_BLOCK, MIX_W), lambda b, q: (b * nb + q, 0)),
        out_shape=jax.ShapeDtypeStruct((batch * seq, MIX_W), BF16),
        compiler_params=_cparams("arbitrary", "arbitrary"),
        name="latent_attention",
    )(qr, kr, kr, kr, z, z, z, cache_k, cache_v, sink_row)


def _gated_head_norm(total, gate, nw, act):
    y = total * lax.rsqrt(jnp.mean(total * total, axis=-1, keepdims=True) + EPS)
    return (act(gate) * (y * nw)).astype(BF16)


def _tri_mask(n, rev):
    t = lax.broadcasted_iota(jnp.int32, (n, n), 0)
    s = lax.broadcasted_iota(jnp.int32, (n, n), 1)
    return (s >= t) if rev else (s <= t)


def _mlstm_kernel(*refs, rev, d, combine):
    if combine:
        (q_ref, k_ref, v_ref, t_ref, bias_ref, c0_ref, n0_ref, m0_ref, hp_ref, g_ref, nw_ref,
         o_ref, c_ref, n_ref, m_ref) = refs
    else:
        (q_ref, k_ref, v_ref, t_ref, bias_ref, c0_ref, n0_ref, m0_ref,
         o_ref, c_ref, n_ref, m_ref) = refs
    L = SCAN_CHUNK

    @pl.when(pl.program_id(1) == 0)
    def _():
        c_ref[...] = c0_ref[...]
        n_ref[...] = n0_ref[...]
        m_ref[...] = m0_ref[...]

    gates = t_ref[...] + bias_ref[...]
    tri = _tri_mask(L, rev)
    tri_b = tri.astype(BF16)
    hi, mid, lo = _split3(_log_sigmoid(gates))
    cum = _dot(tri_b, hi) + _dot(tri_b, mid) + _dot(tri_b, lo)
    gates_t = gates.T
    cum_t = cum.T
    last = 0 if rev else L - 1
    scale = HEAD_DIM ** -0.5
    for h in range(N_HEADS):
        ci, cf = d * 8 + h, d * 8 + 4 + h
        sl = slice(h * HEAD_DIM, (h + 1) * HEAD_DIM)
        i_col, i_row = gates[:, ci:ci + 1], gates_t[ci:ci + 1, :]
        b_col, b_row = cum[:, cf:cf + 1], cum_t[cf:cf + 1, :]
        m = m_ref[0, :, h:h + 1]
        a = b_col + m
        dmat = jnp.where(tri, b_col - b_row + i_row, -jnp.inf)
        mt = jnp.maximum(a, jnp.max(dmat, axis=1, keepdims=True))
        w_state = jnp.exp(a - mt)
        qb = q_ref[:, sl].astype(BF16)
        ks = k_ref[:, sl] * scale
        kb = ks.astype(BF16)
        vb = v_ref[:, sl].astype(BF16)
        s = _dot_nt(qb, kb) * jnp.exp(dmat - mt)
        c_old = c_ref[0, h]
        n_old = n_ref[0, h:h + 1, :]
        num = w_state * _dot(qb, c_old.astype(BF16)) + _dot(s.astype(BF16), vb)
        qn = jnp.sum(qb.astype(F32) * n_old.astype(BF16).astype(F32), axis=1, keepdims=True)
        den = w_state * qn + jnp.sum(s, axis=1, keepdims=True)
        hh = num / jnp.maximum(jnp.abs(den), jnp.exp(-mt))
        if combine:
            o_ref[:, sl] = _gated_head_norm(hh + hp_ref[:, sl], g_ref[:, sl], nw_ref[:, sl], jax.nn.sigmoid)
        else:
            o_ref[:, sl] = hh
        bl = b_col[last:last + 1, :]
        g = bl - b_col + i_col
        m_new = jnp.maximum(bl + m, jnp.max(g, axis=0, keepdims=True))
        w_old = jnp.exp(bl + m - m_new)
        kw = jnp.exp(g - m_new) * ks
        c_ref[0, h] = w_old * c_old + _dot(kw.T.astype(BF16), vb)
        n_ref[0, h:h + 1, :] = w_old * n_old + jnp.sum(kw, axis=0, keepdims=True)
        m_ref[0, :, h:h + 1] = m_new


def _chunk_index(rev, n):
    return (lambda c: n - 1 - c) if rev else (lambda c: c)


def mlstm_direction(z, bias_row, c0, n0, m0, *, d, row0, batch, seq, prev=None, norm_w=None):
    L = SCAN_CHUNK
    nc = seq // L
    r0 = row0 // L
    rev = d == 1
    cidx = _chunk_index(rev, nc)
    combine = prev is not None

    def zspec(col):
        return pl.BlockSpec((L, MIX_W), lambda b, c: (r0 + b * nc + cidx(c), col // MIX_W))

    in_specs = [zspec(Z_MQ), zspec(Z_MK), zspec(Z_MV),
                pl.BlockSpec((L, 128), lambda b, c: (r0 + b * nc + cidx(c), Z_TAIL // 128)),
                pl.BlockSpec((1, 128), lambda b, c: (0, 0)),
                pl.BlockSpec((1, N_HEADS, HEAD_DIM, HEAD_DIM), lambda b, c: (b, 0, 0, 0)),
                pl.BlockSpec((1, N_HEADS, HEAD_DIM), lambda b, c: (b, 0, 0)),
                pl.BlockSpec((1, 1, N_HEADS), lambda b, c: (b, 0, 0))]
    args = [z, z, z, z, bias_row, c0, n0, m0]
    if combine:
        in_specs += [pl.BlockSpec((L, MIX_W), lambda b, c: (b * nc + cidx(c), 0)),
                     zspec(Z_MO),
                     pl.BlockSpec((1, MIX_W), lambda b, c: (0, 0))]
        args += [prev, z, norm_w]
    return pl.pallas_call(
        functools.partial(_mlstm_kernel, rev=rev, d=d, combine=combine),
        grid=(batch, nc),
        in_specs=in_specs,
        out_specs=[pl.BlockSpec((L, MIX_W), lambda b, c: (b * nc + cidx(c), 0)),
                   pl.BlockSpec((1, N_HEADS, HEAD_DIM, HEAD_DIM), lambda b, c: (b, 0, 0, 0)),
                   pl.BlockSpec((1, N_HEADS, HEAD_DIM), lambda b, c: (b, 0, 0)),
                   pl.BlockSpec((1, 1, N_HEADS), lambda b, c: (b, 0, 0))],
        out_shape=[jax.ShapeDtypeStruct((batch * seq, MIX_W), BF16 if combine else F32),
                   jax.ShapeDtypeStruct((batch, N_HEADS, HEAD_DIM, HEAD_DIM), F32),
                   jax.ShapeDtypeStruct((batch, N_HEADS, HEAD_DIM), F32),
                   jax.ShapeDtypeStruct((batch, 1, N_HEADS), F32)],
        compiler_params=_cparams("arbitrary", "arbitrary"),
        name="mlstm_scan",
    )(*args)


def _retention_kernel(*refs, rev, combine):
    if combine:
        q_ref, k_ref, v_ref, dec_ref, s0_ref, hp_ref, g_ref, nw_ref, o_ref, s_ref = refs
    else:
        q_ref, k_ref, v_ref, dec_ref, s0_ref, o_ref, s_ref = refs
    L = SCAN_CHUNK

    @pl.when(pl.program_id(1) == 0)
    def _():
        s_ref[...] = s0_ref[...]

    tri = _tri_mask(L, rev)
    t_i = lax.broadcasted_iota(jnp.int32, (L, L), 0)
    s_i = lax.broadcasted_iota(jnp.int32, (L, L), 1)
    dist = jnp.abs(t_i - s_i).astype(F32)
    pos = lax.broadcasted_iota(jnp.int32, (L, 1), 0).astype(F32)
    xi_pow = (L - pos) if rev else (pos + 1.0)
    zeta_pow = pos if rev else (L - 1.0 - pos)
    log_gamma = _log_sigmoid(dec_ref[...])
    scale = HEAD_DIM ** -0.5
    for h in range(N_HEADS):
        sl = slice(h * HEAD_DIM, (h + 1) * HEAD_DIM)
        lg = log_gamma[:, h:h + 1]
        decay = jnp.where(tri, jnp.exp(dist * lg), 0.0)
        qb = q_ref[:, sl].astype(BF16)
        ks = k_ref[:, sl] * scale
        vb = v_ref[:, sl].astype(BF16)
        att = _dot_nt(qb, ks.astype(BF16)) * decay
        s_old = s_ref[0, h]
        y = _dot(att.astype(BF16), vb) + jnp.exp(xi_pow * lg) * _dot(qb, s_old.astype(BF16))
        if combine:
            o_ref[:, sl] = _gated_head_norm(y + hp_ref[:, sl], g_ref[:, sl], nw_ref[:, sl], jax.nn.silu)
        else:
            o_ref[:, sl] = y
        kz = ks * jnp.exp(zeta_pow * lg)
        s_ref[0, h] = jnp.exp(L * lg) * s_old + _dot(kz.T.astype(BF16), vb)


def retention_direction(z, decay_row, s0, *, d, row0, batch, seq, prev=None, norm_w=None):
    L = SCAN_CHUNK
    nc = seq // L
    r0 = row0 // L
    rev = d == 1
    cidx = _chunk_index(rev, nc)
    combine = prev is not None

    def zspec(col):
        return pl.BlockSpec((L, MIX_W), lambda b, c: (r0 + b * nc + cidx(c), col // MIX_W))

    in_specs = [zspec(Z_RQ), zspec(Z_RK), zspec(Z_RV),
                pl.BlockSpec((1, 128), lambda b, c: (0, 0)),
                pl.BlockSpec((1, N_HEADS, HEAD_DIM, HEAD_DIM), lambda b, c: (b, 0, 0, 0))]
    args = [z, z, z, decay_row, s0]
    if combine:
        in_specs += [pl.BlockSpec((L, MIX_W), lambda b, c: (b * nc + cidx(c), 0)),
                     zspec(Z_RG),
                     pl.BlockSpec((1, MIX_W), lambda b, c: (0, 0))]
        args += [prev, z, norm_w]
    return pl.pallas_call(
        functools.partial(_retention_kernel, rev=rev, combine=combine),
        grid=(batch, nc),
        in_specs=in_specs,
        out_specs=[pl.BlockSpec((L, MIX_W), lambda b, c: (b * nc + cidx(c), 0)),
                   pl.BlockSpec((1, N_HEADS, HEAD_DIM, HEAD_DIM), lambda b, c: (b, 0, 0, 0))],
        out_shape=[jax.ShapeDtypeStruct((batch * seq, MIX_W), BF16 if combine else F32),
                   jax.ShapeDtypeStruct((batch, N_HEADS, HEAD_DIM, HEAD_DIM), F32)],
        compiler_params=_cparams("arbitrary", "arbitrary"),
        name="retention_scan",
    )(*args)


GLA_LEVELS = (32, 16, 8, 4, 2, 1)
GLA_LANES = N_HEADS * G_DK


def _gla_constants(rev):
    L = GLA_CHUNK
    t = np.arange(L)[:, None]
    r = np.arange(L)[None, :]
    mats = [(r <= t).astype(np.float32)]
    pair = [np.eye(L, dtype=np.float32)]
    for w in GLA_LEVELS:
        blk_t, blk_r = t // (2 * w), r // (2 * w)
        ref_t = blk_t * 2 * w + w - 1
        upper_t = (t % (2 * w)) >= w
        mats.append((upper_t & (r > ref_t) & (r <= t)).astype(np.float32))
        mats.append(((~upper_t) & (r > t) & (r <= ref_t)).astype(np.float32))
        lower_r = (r % (2 * w)) < w
        pair.append((upper_t & lower_r & (blk_t == blk_r)).astype(np.float32))
    if rev:
        mats = [m[::-1, ::-1] for m in mats]
        pair = [p[::-1, ::-1] for p in pair]
    big = np.concatenate(mats, axis=0)
    big3 = np.concatenate([big, big, big], axis=1)
    pair = np.stack([np.tile(p, (1, N_HEADS)) for p in pair])
    return jnp.asarray(big3, BF16), jnp.asarray(pair, F32)


def _gla_block_masks():
    L = GLA_CHUNK
    rows = np.arange(N_HEADS * L)[:, None] // L
    kmask = (rows == np.arange(GLA_LANES)[None, :] // G_DK).astype(np.float32)
    vmask = (rows == np.arange(MIX_W)[None, :] // HEAD_DIM).astype(np.float32)
    srow = np.arange(GLA_LANES)[:, None] // G_DK
    smask = (srow == np.arange(MIX_W)[None, :] // HEAD_DIM).astype(np.float32)
    return jnp.asarray(kmask, F32), jnp.asarray(vmask, F32), jnp.asarray(smask, F32)


def _gla_kernel(*refs, rev, combine):
    if combine:
        (q_ref, k_ref, v_ref, t_ref, w2_ref, gb_ref, big_ref, pair_ref, km_ref, vm_ref, sm_ref, s0_ref,
         hp_ref, g_ref, nw_ref, o_ref, so_ref, s_ref) = refs
    else:
        (q_ref, k_ref, v_ref, t_ref, w2_ref, gb_ref, big_ref, pair_ref, km_ref, vm_ref, sm_ref, s0_ref,
         o_ref, so_ref, s_ref) = refs
    L = GLA_CHUNK
    c = pl.program_id(1)

    @pl.when(c == 0)
    def _():
        s_ref[...] = jnp.zeros_like(s_ref)
        for h in range(N_HEADS):
            s_ref[h * G_DK:(h + 1) * G_DK, h * HEAD_DIM:(h + 1) * HEAD_DIM] = s0_ref[0, h]

    logit = _dot(t_ref[...].astype(BF16), w2_ref[...]) + gb_ref[...]
    la = _log_sigmoid(logit) * (1.0 / GLA_TAU)
    e = _dot(big_ref[...], jnp.concatenate(_split3(la), axis=0))
    cum = e[0:L]
    q = q_ref[...] * (G_DK ** -0.5)
    k = k_ref[...]
    kmask = km_ref[...]

    def stacked(x):
        return (jnp.concatenate([x] * N_HEADS, axis=0) * kmask).astype(BF16)

    att = pair_ref[0] * _dot_nt(q.astype(BF16), stacked(k))
    for i in range(len(GLA_LEVELS)):
        eq = e[(2 * i + 1) * L:(2 * i + 2) * L]
        ek = e[(2 * i + 2) * L:(2 * i + 3) * L]
        att = att + pair_ref[i + 1] * _dot_nt((q * jnp.exp(eq)).astype(BF16), stacked(k * jnp.exp(ek)))

    v = v_ref[...]
    vblk = (jnp.concatenate([v] * N_HEADS, axis=0) * vm_ref[...]).astype(BF16)
    s_old = s_ref[...]
    y = _dot(att.astype(BF16), vblk) + _dot((q * jnp.exp(cum)).astype(BF16), s_old.astype(BF16))
    if combine:
        for h in range(N_HEADS):
            sl = slice(h * HEAD_DIM, (h + 1) * HEAD_DIM)
            o_ref[:, sl] = _gated_head_norm(y[:, sl] + hp_ref[:, sl], g_ref[:, sl], nw_ref[:, sl], jax.nn.silu)
    else:
        o_ref[...] = y

    last = 0 if rev else L - 1
    last_row = cum[last:last + 1, :]
    pad = jnp.zeros((128 - L, GLA_LANES), F32)
    kd_t = jnp.concatenate([k * jnp.exp(last_row - cum), pad], axis=0).T
    cum_t = jnp.concatenate([cum, pad], axis=0).T
    v_pad = jnp.concatenate([v, jnp.zeros((128 - L, MIX_W), F32)], axis=0).astype(BF16)
    s_new = jnp.exp(cum_t[:, last:last + 1]) * s_old + sm_ref[...] * _dot(kd_t.astype(BF16), v_pad)
    s_ref[...] = s_new

    @pl.when(c == pl.num_programs(1) - 1)
    def _():
        for h in range(N_HEADS):
            so_ref[0, h] = s_new[h * G_DK:(h + 1) * G_DK, h * HEAD_DIM:(h + 1) * HEAD_DIM]


def gla_direction(z, w2pad, gbias, s0, *, d, row0, batch, seq, prev=None, norm_w=None):
    L = GLA_CHUNK
    nc = seq // L
    r0 = row0 // L
    rev = d == 1
    cidx = _chunk_index(rev, nc)
    combine = prev is not None
    big3, pair = _gla_constants(rev)
    kmask, vmask, smask = _gla_block_masks()

    def const(a):
        return pl.BlockSpec(a.shape, lambda b, c: (0,) * a.ndim)

    in_specs = [pl.BlockSpec((L, GLA_LANES), lambda b, c: (r0 + b * nc + cidx(c), Z_GQ // GLA_LANES)),
                pl.BlockSpec((L, GLA_LANES), lambda b, c: (r0 + b * nc + cidx(c), Z_GK // GLA_LANES)),
                pl.BlockSpec((L, MIX_W), lambda b, c: (r0 + b * nc + cidx(c), Z_GV // MIX_W)),
                pl.BlockSpec((L, 128), lambda b, c: (r0 + b * nc + cidx(c), Z_TAIL // 128)),
                const(w2pad), const(gbias), const(big3), const(pair), const(kmask), const(vmask), const(smask),
                pl.BlockSpec((1, N_HEADS, G_DK, HEAD_DIM), lambda b, c: (b, 0, 0, 0))]
    args = [z, z, z, z, w2pad, gbias, big3, pair, kmask, vmask, smask, s0]
    if combine:
        in_specs += [pl.BlockSpec((L, MIX_W), lambda b, c: (b * nc + cidx(c), 0)),
                     pl.BlockSpec((L, MIX_W), lambda b, c: (r0 + b * nc + cidx(c), Z_GG // MIX_W)),
                     pl.BlockSpec((1, MIX_W), lambda b, c: (0, 0))]
        args += [prev, z, norm_w]
    return pl.pallas_call(
        functools.partial(_gla_kernel, rev=rev, combine=combine),
        grid=(batch, nc),
        in_specs=in_specs,
        out_specs=[pl.BlockSpec((L, MIX_W), lambda b, c: (b * nc + cidx(c), 0)),
                   pl.BlockSpec((1, N_HEADS, G_DK, HEAD_DIM), lambda b, c: (b, 0, 0, 0))],
        out_shape=[jax.ShapeDtypeStruct((batch * seq, MIX_W), BF16 if combine else F32),
                   jax.ShapeDtypeStruct((batch, N_HEADS, G_DK, HEAD_DIM), F32)],
        scratch_shapes=[pltpu.VMEM((GLA_LANES, MIX_W), F32)],
        compiler_params=_cparams("arbitrary", "arbitrary"),
        name="gla_scan",
    )(*args)


def _pad_row(v, width=128):
    v = v.reshape(1, -1).astype(F32)
    return jnp.pad(v, ((0, 0), (0, width - v.shape[1])))


def _permute_w_in(w_in_l):
    sizes = (512, 256, 256, 512, 512, 512, 512, 16, 512, 512, 512, 512, 256, 256, 512, 512, 32)
    offs = np.concatenate([[0], np.cumsum(sizes)])
    pieces = [w_in_l[:, offs[i]:offs[i + 1]] for i in range(len(sizes))]
    order = [0, 1, 2, 3, 4, 5, 6, 8, 9, 10, 11, 12, 13, 14, 15, 7, 16]
    w = jnp.concatenate([pieces[i] for i in order], axis=1)
    return jnp.pad(w, ((0, 0), (0, Z_COLS - w.shape[1]))).astype(BF16)


def _mixers(z, p, l, *, group_rows, ctx_dims, lat_dims, cache):
    (cb, ct), (lb, lt) = ctx_dims, lat_dims
    sink_row = _pad_row(p["attn_sink"][l])
    ya_ctx = context_attention(z, sink_row, batch=cb, seq=ct)
    cos, sin = rope_tables(lt)
    qr, kr = rope_qk(z, cos, sin, row0=group_rows, batch=lb, seq=lt, tb=min(lt, 512))
    ck = cache["attn_k"].reshape(cache["attn_k"].shape[:3] + (A_KV * HEAD_DIM,))
    cv = cache["attn_v"].reshape(cache["attn_v"].shape[:3] + (A_KV * HEAD_DIM,))
    ya_lat = latent_attention(qr, kr, z, ck, cv, sink_row, layer=l, row0=group_rows, batch=lb, seq=lt)
    ys = [jnp.concatenate([ya_ctx, ya_lat], axis=0)]
    states = {}

    bias_row = _pad_row(p["mlstm_if_b"][l])
    nw = p["mlstm_norm_w"][l].reshape(1, MIX_W)
    outs = []
    for name, (bsz, seq), row0 in (("ctx", ctx_dims, 0), ("lat", lat_dims, group_rows)):
        res = {}
        for d in (1, 0):
            if name == "ctx":
                c0 = jnp.zeros((bsz, N_HEADS, HEAD_DIM, HEAD_DIM), F32)
                n0 = jnp.zeros((bsz, N_HEADS, HEAD_DIM), F32)
                m0 = jnp.zeros((bsz, 1, N_HEADS), F32)
            else:
                c0 = cache["mlstm_C"][:, l, d]
                n0 = cache["mlstm_n"][:, l, d]
                m0 = cache["mlstm_m"][:, l, d].reshape(bsz, 1, N_HEADS)
            res[d] = mlstm_direction(z, bias_row, c0, n0, m0, d=d, row0=row0, batch=bsz, seq=seq,
                                     prev=res[1][0] if d == 0 else None, norm_w=nw if d == 0 else None)
        outs.append(res[0][0])
        if name == "ctx":
            states["mlstm_C"] = jnp.stack([res[0][1], res[1][1]], axis=1)
            states["mlstm_n"] = jnp.stack([res[0][2], res[1][2]], axis=1)
            states["mlstm_m"] = jnp.stack([res[0][3][:, 0], res[1][3][:, 0]], axis=1)
    ys.append(jnp.concatenate(outs, axis=0))

    nw = p["ret_norm_w"][l].reshape(1, MIX_W)
    outs = []
    for name, (bsz, seq), row0 in (("ctx", ctx_dims, 0), ("lat", lat_dims, group_rows)):
        res = {}
        for d in (1, 0):
            s0 = (jnp.zeros((bsz, N_HEADS, HEAD_DIM, HEAD_DIM), F32) if name == "ctx"
                  else cache["ret_S"][:, l, d])
            res[d] = retention_direction(z, _pad_row(p["ret_decay"][l, d]), s0, d=d, row0=row0, batch=bsz,
                                         seq=seq, prev=res[1][0] if d == 0 else None,
                                         norm_w=nw if d == 0 else None)
        outs.append(res[0][0])
        if name == "ctx":
            states["ret_S"] = jnp.stack([res[0][1], res[1][1]], axis=1)
    ys.append(jnp.concatenate(outs, axis=0))

    nw = p["gla_norm_w"][l].reshape(1, MIX_W)
    outs = []
    for name, (bsz, seq), row0 in (("ctx", ctx_dims, 0), ("lat", lat_dims, group_rows)):
        res = {}
        for d in (1, 0):
            lo = TAIL_GLR + d * G_RANK
            w2pad = jnp.zeros((128, GLA_LANES), F32).at[lo:lo + G_RANK].set(p["gla_w2"][l, d]).astype(BF16)
            gbias = p["gla_b"][l, d].reshape(1, GLA_LANES)
            s0 = (jnp.zeros((bsz, N_HEADS, G_DK, HEAD_DIM), F32) if name == "ctx"
                  else cache["gla_S"][:, l, d])
            res[d] = gla_direction(z, w2pad, gbias, s0, d=d, row0=row0, batch=bsz, seq=seq,
                                   prev=res[1][0] if d == 0 else None, norm_w=nw if d == 0 else None)
        outs.append(res[0][0])
        if name == "ctx":
            states["gla_S"] = jnp.stack([res[0][1], res[1][1]], axis=1)
    ys.append(jnp.concatenate(outs, axis=0))
    return ys, states


def _layer(x, mod_l, p, l, *, group_rows, ctx_dims, lat_dims, cache, tm):
    d = x.shape[1]
    sh1, sc1, g1, sh2, sc2, g2 = (mod_l[:, i * d:(i + 1) * d] for i in range(6))
    z, h = in_projection(x, sh1, sc1, p["norm1_w"][l].reshape(1, d), _permute_w_in(p["w_in"][l]),
                         group_rows=group_rows, tm=tm)
    ys, states = _mixers(z, p, l, group_rows=group_rows, ctx_dims=ctx_dims, lat_dims=lat_dims, cache=cache)
    merged = merge_branches(h, ys, p["w_mgate"][l].astype(BF16), p["w_br"][l].astype(BF16), tm=tm, tn=256)
    x = residual_projection(merged, p["w_out"][l].astype(BF16), x, g1, group_rows=group_rows, tm=tm, tn=512)
    act = ffn_up(x, sh2, sc2, p["norm2_w"][l].reshape(1, d), p["ffn_w_gu"][l].astype(BF16),
                 group_rows=group_rows, tm=tm, tn=512)
    x = residual_projection(act, p["ffn_w_down"][l].astype(BF16), x, g2, group_rows=group_rows, tm=tm, tn=256)
    ct = ctx_dims[1]
    states["attn_k"] = z[:group_rows, Z_AK:Z_AK + 256].reshape(ctx_dims[0], ct, A_KV, HEAD_DIM)
    states["attn_v"] = z[:group_rows, Z_AV:Z_AV + 256].reshape(ctx_dims[0], ct, A_KV, HEAD_DIM)
    return x, states


def _forward(x_prompt, x_sample, cache, c, c_ctx, p, final_norm_w, *, tm):
    cb, ct, d = x_prompt.shape
    lb, lt, _ = x_sample.shape
    group_rows = cb * ct
    assert group_rows == lt and group_rows % tm == 0 and lb + 1 <= MOD_ROWS
    depth = p["w_ada"].shape[0]
    x = jnp.concatenate([x_prompt.reshape(group_rows, d), x_sample.reshape(lb * lt, d)], axis=0)
    cond = jnp.concatenate([c_ctx[None, :], c, jnp.zeros((MOD_ROWS - 1 - lb, d), F32)], axis=0)
    mod = ada_modulation(cond, p["w_ada"], p["b_ada"])
    all_states = []
    for l in range(depth):
        x, st = _layer(x, mod[l], p, l, group_rows=group_rows, ctx_dims=(cb, ct), lat_dims=(lb, lt),
                       cache=cache, tm=tm)
        all_states.append(st)
    y = final_norm(x, final_norm_w.reshape(1, d), tm=tm)
    y_prompt = y[:group_rows].reshape(cb, ct, d)
    y_sample = y[group_rows:].reshape(lb, lt, d)

    def stack(name):
        return jnp.stack([s[name] for s in all_states], axis=1)

    return (y_prompt, y_sample, stack("attn_k"), stack("attn_v"), stack("mlstm_C"), stack("mlstm_n"),
            stack("mlstm_m"), stack("ret_S"), stack("gla_S"))


def kernel(x_prompt, x_sample, cache_attn_k, cache_attn_v, state_mlstm_C, state_mlstm_n, state_mlstm_m, state_ret_S, state_gla_S, c, c_ctx, w_ada, b_ada, norm1_w, norm2_w, w_in, attn_sink, mlstm_if_b, mlstm_norm_w, ret_decay, ret_norm_w, gla_w2, gla_b, gla_norm_w, w_br, w_mgate, w_out, ffn_w_gu, ffn_w_down, final_norm_w):
    p = {"w_ada": w_ada, "b_ada": b_ada, "norm1_w": norm1_w, "norm2_w": norm2_w, "w_in": w_in,
         "attn_sink": attn_sink, "mlstm_if_b": mlstm_if_b, "mlstm_norm_w": mlstm_norm_w,
         "ret_decay": ret_decay, "ret_norm_w": ret_norm_w, "gla_w2": gla_w2, "gla_b": gla_b,
         "gla_norm_w": gla_norm_w, "w_br": w_br, "w_mgate": w_mgate, "w_out": w_out,
         "ffn_w_gu": ffn_w_gu, "ffn_w_down": ffn_w_down}
    cache = {"attn_k": cache_attn_k, "attn_v": cache_attn_v, "mlstm_C": state_mlstm_C,
             "mlstm_n": state_mlstm_n, "mlstm_m": state_mlstm_m, "ret_S": state_ret_S, "gla_S": state_gla_S}
    return _forward(x_prompt, x_sample, cache, c, c_ctx, p, final_norm_w, tm=1024)
```

```python
import functools

import numpy as np
import jax
import jax.numpy as jnp
from jax import lax
from jax.experimental import pallas as pl
from jax.experimental.pallas import tpu as pltpu

F32 = jnp.float32
BF16 = jnp.bfloat16

HEAD_DIM = 128
N_HEADS = 4
A_KV = 2
GRID_W = 64
WINDOW = 128
Q_BLOCK = 128
G_DK = 64
G_RANK = 16
GLA_TAU = 16.0
ROPE_BASE = 10000.0
EPS = 1e-6
MOD_ROWS = 16
SCAN_CHUNK = 128
GLA_CHUNK = 64
SCAN_SEQS = 4
MIX_W = N_HEADS * HEAD_DIM
KV_W = A_KV * HEAD_DIM

Z_AQ, Z_AK, Z_AV = 0, 512, 768
Z_MQ, Z_MK, Z_MV, Z_MO = 1024, 1536, 2048, 2560
Z_RQ, Z_RK, Z_RV, Z_RG = 3072, 3584, 4096, 4608
Z_GQ, Z_GK, Z_GV, Z_GG = 5120, 5376, 5632, 6144
Z_TAIL = 6656
Z_COLS = 6912
Z_TN = 768
TAIL_GLR = 16

VMEM_LIMIT = 56 * 2 ** 20


def _cparams(*sem):
    return pltpu.CompilerParams(dimension_semantics=sem, vmem_limit_bytes=VMEM_LIMIT)


def _log_sigmoid(x):
    return jnp.minimum(x, 0.0) - jnp.log1p(jnp.exp(-jnp.abs(x)))


def _split3(x):
    hi = x.astype(BF16)
    r = x - hi.astype(F32)
    mid = r.astype(BF16)
    lo = (r - mid.astype(F32)).astype(BF16)
    return hi, mid, lo


def _dot(a, b):
    return jnp.dot(a, b, preferred_element_type=F32)


def _dot_nt(a, b):
    return lax.dot_general(a, b, (((1,), (1,)), ((), ())), preferred_element_type=F32)


def _ada_kernel(cond_ref, w_ref, b_ref, o_ref):
    s = jax.nn.silu(cond_ref[...]).astype(BF16)
    o_ref[0] = _dot(s, w_ref[0].astype(BF16)) + b_ref[0]


def ada_modulation(cond, w_ada, b_ada):
    depth, d, n = w_ada.shape
    tn = 1024
    return pl.pallas_call(
        _ada_kernel,
        grid=(depth, n // tn),
        in_specs=[pl.BlockSpec((MOD_ROWS, d), lambda l, j: (0, 0)),
                  pl.BlockSpec((1, d, tn), lambda l, j: (l, 0, j)),
                  pl.BlockSpec((1, 1, tn), lambda l, j: (l, 0, j))],
        out_specs=pl.BlockSpec((1, MOD_ROWS, tn), lambda l, j: (l, 0, j)),
        out_shape=jax.ShapeDtypeStruct((depth, MOD_ROWS, n), F32),
        compiler_params=_cparams("arbitrary", "arbitrary"),
        name="ada_modulation",
    )(cond, w_ada, b_ada.reshape(depth, 1, n))


def _mod_row(grp, tm):
    return grp[0] + (pl.program_id(0) * tm) // grp[1]


def _modulated_norm(x, nw, sh, sc):
    y = x * lax.rsqrt(jnp.mean(x * x, axis=-1, keepdims=True) + EPS)
    return (y * nw) * (1.0 + sc) + sh


def _in_proj_kernel(x_ref, sh_ref, sc_ref, nw_ref, w_ref, z_ref, h_ref, *, tm, grp):
    @pl.when(pl.program_id(1) == 0)
    def _():
        g = _mod_row(grp, tm)
        h = _modulated_norm(x_ref[...], nw_ref[...], sh_ref[pl.ds(g, 1), :], sc_ref[pl.ds(g, 1), :])
        h_ref[...] = h.astype(BF16)

    z_ref[...] = _dot(h_ref[...], w_ref[...])


def in_projection(x, sh, sc, nw, w, *, grp, tm):
    rows, d = x.shape
    n = w.shape[1]
    return pl.pallas_call(
        functools.partial(_in_proj_kernel, tm=tm, grp=grp),
        grid=(rows // tm, n // Z_TN),
        in_specs=[pl.BlockSpec((tm, d), lambda i, j: (i, 0)),
                  pl.BlockSpec((MOD_ROWS, d), lambda i, j: (0, 0)),
                  pl.BlockSpec((MOD_ROWS, d), lambda i, j: (0, 0)),
                  pl.BlockSpec((1, d), lambda i, j: (0, 0)),
                  pl.BlockSpec((d, Z_TN), lambda i, j: (0, j))],
        out_specs=[pl.BlockSpec((tm, Z_TN), lambda i, j: (i, j)),
                   pl.BlockSpec((tm, d), lambda i, j: (i, 0))],
        out_shape=[jax.ShapeDtypeStruct((rows, n), F32), jax.ShapeDtypeStruct((rows, d), BF16)],
        compiler_params=_cparams("arbitrary", "arbitrary"),
        name="in_projection",
    )(x, sh, sc, nw, w)


def _merge_kernel(h_ref, ya_ref, ym_ref, yr_ref, yg_ref, wg_ref, wb_ref, o_ref):
    h = h_ref[...]
    acc = None
    for b, y_ref in enumerate((ya_ref, ym_ref, yr_ref, yg_ref)):
        gate = jax.nn.sigmoid(_dot(h, wg_ref[b]))
        term = gate * _dot(y_ref[...], wb_ref[b])
        acc = term if acc is None else acc + term
    o_ref[...] = acc.astype(BF16)


def merge_branches(h, ys, w_mgate, w_br, *, tm, tn):
    rows, d = h.shape
    nb, mw, _ = w_br.shape
    return pl.pallas_call(
        _merge_kernel,
        grid=(rows // tm, d // tn),
        in_specs=[pl.BlockSpec((tm, d), lambda i, j: (i, 0))]
        + [pl.BlockSpec((tm, mw), lambda i, j: (i, 0))] * nb
        + [pl.BlockSpec((nb, d, tn), lambda i, j: (0, 0, j)),
           pl.BlockSpec((nb, mw, tn), lambda i, j: (0, 0, j))],
        out_specs=pl.BlockSpec((tm, tn), lambda i, j: (i, j)),
        out_shape=jax.ShapeDtypeStruct((rows, d), BF16),
        compiler_params=_cparams("arbitrary", "arbitrary"),
        name="merge_branches",
    )(h, *ys, w_mgate, w_br)


def _residual_proj_kernel(a_ref, w_ref, x_ref, g_ref, o_ref, *, tm, grp):
    g = _mod_row(grp, tm)
    o_ref[...] = x_ref[...] + g_ref[pl.ds(g, 1), :] * _dot(a_ref[...], w_ref[...])


def residual_projection(a, w, x, gate, *, grp, tm, tn):
    rows, k = a.shape
    n = w.shape[1]
    return pl.pallas_call(
        functools.partial(_residual_proj_kernel, tm=tm, grp=grp),
        grid=(rows // tm, n // tn),
        in_specs=[pl.BlockSpec((tm, k), lambda i, j: (i, 0)),
                  pl.BlockSpec((k, tn), lambda i, j: (0, j)),
                  pl.BlockSpec((tm, tn), lambda i, j: (i, j)),
                  pl.BlockSpec((MOD_ROWS, tn), lambda i, j: (0, j))],
        out_specs=pl.BlockSpec((tm, tn), lambda i, j: (i, j)),
        out_shape=jax.ShapeDtypeStruct((rows, n), F32),
        compiler_params=_cparams("arbitrary", "arbitrary"),
        name="residual_projection",
    )(a, w, x, gate)


def _ffn_up_kernel(x_ref, sh_ref, sc_ref, nw_ref, wg_ref, wu_ref, o_ref, h_ref, *, tm, grp):
    @pl.when(pl.program_id(1) == 0)
    def _():
        g = _mod_row(grp, tm)
        h = _modulated_norm(x_ref[...], nw_ref[...], sh_ref[pl.ds(g, 1), :], sc_ref[pl.ds(g, 1), :])
        h_ref[...] = h.astype(BF16)

    h = h_ref[...]
    o_ref[...] = (jax.nn.silu(_dot(h, wg_ref[...])) * _dot(h, wu_ref[...])).astype(BF16)


def ffn_up(x, sh, sc, nw, w_gu, *, grp, tm, tn):
    rows, d = x.shape
    hid = w_gu.shape[1] // 2
    nj = hid // tn
    return pl.pallas_call(
        functools.partial(_ffn_up_kernel, tm=tm, grp=grp),
        grid=(rows // tm, nj),
        in_specs=[pl.BlockSpec((tm, d), lambda i, j: (i, 0)),
                  pl.BlockSpec((MOD_ROWS, d), lambda i, j: (0, 0)),
                  pl.BlockSpec((MOD_ROWS, d), lambda i, j: (0, 0)),
                  pl.BlockSpec((1, d), lambda i, j: (0, 0)),
                  pl.BlockSpec((d, tn), lambda i, j: (0, j)),
                  pl.BlockSpec((d, tn), lambda i, j: (0, j + nj))],
        out_specs=pl.BlockSpec((tm, tn), lambda i, j: (i, j)),
        out_shape=jax.ShapeDtypeStruct((rows, hid), BF16),
        scratch_shapes=[pltpu.VMEM((tm, d), BF16)],
        compiler_params=_cparams("arbitrary", "arbitrary"),
        name="ffn_up",
    )(x, sh, sc, nw, w_gu, w_gu)


def _final_norm_kernel(x_ref, w_ref, o_ref):
    x = x_ref[...]
    o_ref[...] = x * lax.rsqrt(jnp.mean(x * x, axis=-1, keepdims=True) + EPS) * w_ref[...]


def final_norm(x, w, *, tm):
    rows, d = x.shape
    return pl.pallas_call(
        _final_norm_kernel,
        grid=(rows // tm,),
        in_specs=[pl.BlockSpec((tm, d), lambda i: (i, 0)), pl.BlockSpec((1, d), lambda i: (0, 0))],
        out_specs=pl.BlockSpec((tm, d), lambda i: (i, 0)),
        out_shape=jax.ShapeDtypeStruct((rows, d), F32),
        compiler_params=_cparams("arbitrary"),
        name="final_norm",
    )(x, w)


def _sink_softmax(s, sink_col):
    m = jnp.maximum(jnp.max(s, axis=-1, keepdims=True), sink_col)
    p = jnp.exp(s - m)
    return p / (jnp.sum(p, axis=-1, keepdims=True) + jnp.exp(sink_col - m))


def _sink_column(sink_ref, kv, rows):
    half = lax.broadcasted_iota(jnp.int32, (2 * rows, 1), 0) < rows
    return jnp.where(half, sink_ref[:, 2 * kv:2 * kv + 1], sink_ref[:, 2 * kv + 1:2 * kv + 2])


def _ctx_attn_kernel(q_ref, k_ref, v_ref, sink_ref, o_ref):
    t = q_ref.shape[0]
    scale = HEAD_DIM ** -0.5
    scores = {}
    for kv in range(A_KV):
        lo = 2 * kv * HEAD_DIM
        q2 = jnp.concatenate([q_ref[:, lo:lo + HEAD_DIM], q_ref[:, lo + HEAD_DIM:lo + 2 * HEAD_DIM]], axis=0)
        k = k_ref[:, kv * HEAD_DIM:(kv + 1) * HEAD_DIM].astype(BF16)
        scores[kv] = _dot_nt(q2.astype(BF16), k) * scale
    for kv in range(A_KV):
        lo = 2 * kv * HEAD_DIM
        v = v_ref[:, kv * HEAD_DIM:(kv + 1) * HEAD_DIM].astype(BF16)
        p = _sink_softmax(scores[kv], _sink_column(sink_ref, kv, t))
        o = _dot(p.astype(BF16), v)
        o_ref[:, lo:lo + HEAD_DIM] = o[:t].astype(BF16)
        o_ref[:, lo + HEAD_DIM:lo + 2 * HEAD_DIM] = o[t:].astype(BF16)


def context_attention(z3, sink_row):
    batch, seq, _ = z3.shape
    return pl.pallas_call(
        _ctx_attn_kernel,
        grid=(batch,),
        in_specs=[pl.BlockSpec((None, seq, MIX_W), lambda b: (b, 0, Z_AQ // MIX_W)),
                  pl.BlockSpec((None, seq, KV_W), lambda b: (b, 0, Z_AK // KV_W)),
                  pl.BlockSpec((None, seq, KV_W), lambda b: (b, 0, Z_AV // KV_W)),
                  pl.BlockSpec((1, 128), lambda b: (0, 0))],
        out_specs=pl.BlockSpec((None, seq, MIX_W), lambda b: (b, 0, 0)),
        out_shape=jax.ShapeDtypeStruct((batch, seq, MIX_W), BF16),
        compiler_params=_cparams("arbitrary"),
        name="context_attention",
    )(z3, z3, z3, sink_row)


def _rope(x, cos, sin_signed):
    lane = lax.broadcasted_iota(jnp.int32, x.shape, 1)
    swapped = jnp.where(lane % 64 < 32, pltpu.roll(x, 96, 1), pltpu.roll(x, 32, 1))
    return x * cos + swapped * sin_signed


def _rope_kernel(q_ref, k_ref, cos_ref, sin_ref, qo_ref, ko_ref):
    cos, sin = cos_ref[...], sin_ref[...]
    for h in range(N_HEADS):
        sl = slice(h * HEAD_DIM, (h + 1) * HEAD_DIM)
        qo_ref[:, sl] = _rope(q_ref[:, sl], cos, sin).astype(BF16)
    for h in range(A_KV):
        sl = slice(h * HEAD_DIM, (h + 1) * HEAD_DIM)
        ko_ref[:, sl] = _rope(k_ref[:, sl], cos, sin).astype(BF16)


def rope_tables(seq):
    pos = np.arange(seq)
    quarter = HEAD_DIM // 4
    inv = jnp.asarray(ROPE_BASE, F32) ** (-jnp.arange(quarter, dtype=F32) / quarter)
    row = jnp.asarray(pos // GRID_W, F32)[:, None] * inv[None, :]
    col = jnp.asarray(pos % GRID_W, F32)[:, None] * inv[None, :]
    cos = jnp.concatenate([jnp.cos(row), jnp.cos(row), jnp.cos(col), jnp.cos(col)], axis=1)
    sin = jnp.concatenate([-jnp.sin(row), jnp.sin(row), -jnp.sin(col), jnp.sin(col)], axis=1)
    return cos, sin


def rope_qk(z3, cos, sin, *, tb):
    batch, seq, _ = z3.shape
    return pl.pallas_call(
        _rope_kernel,
        grid=(batch, seq // tb),
        in_specs=[pl.BlockSpec((None, tb, MIX_W), lambda b, t: (b, t, Z_AQ // MIX_W)),
                  pl.BlockSpec((None, tb, KV_W), lambda b, t: (b, t, Z_AK // KV_W)),
                  pl.BlockSpec((tb, HEAD_DIM), lambda b, t: (t, 0)),
                  pl.BlockSpec((tb, HEAD_DIM), lambda b, t: (t, 0))],
        out_specs=[pl.BlockSpec((None, tb, MIX_W), lambda b, t: (b, t, 0)),
                   pl.BlockSpec((None, tb, KV_W), lambda b, t: (b, t, 0))],
        out_shape=[jax.ShapeDtypeStruct((batch, seq, MIX_W), BF16),
                   jax.ShapeDtypeStruct((batch, seq, KV_W), BF16)],
        compiler_params=_cparams("arbitrary", "arbitrary"),
        name="rope_qk",
    )(z3, z3, cos, sin)


def _lat_attn_kernel(q_ref, kp_ref, kc_ref, kn_ref, vp_ref, vc_ref, vn_ref, kx_ref, vx_ref, sink_ref,
                     o_ref, *, seq):
    qb = pl.program_id(1)
    nq = Q_BLOCK
    past = kx_ref.shape[0]
    scale = HEAD_DIM ** -0.5
    ncol = 3 * nq + past
    jj = lax.broadcasted_iota(jnp.int32, (2 * nq, ncol), 1)
    tt = lax.broadcasted_iota(jnp.int32, (2 * nq, ncol), 0) % nq
    kpos = qb * nq - nq + jj
    band = (jnp.abs(tt + nq - jj) <= WINDOW) & (kpos >= 0) & (kpos < seq)
    ok = (jj >= 3 * nq) | band
    scores = {}
    for kv in range(A_KV):
        lo = 2 * kv * HEAD_DIM
        ks = slice(kv * HEAD_DIM, (kv + 1) * HEAD_DIM)
        q2 = jnp.concatenate([q_ref[:, lo:lo + HEAD_DIM], q_ref[:, lo + HEAD_DIM:lo + 2 * HEAD_DIM]], axis=0)
        kcat = jnp.concatenate([kp_ref[:, ks], kc_ref[:, ks], kn_ref[:, ks],
                                kx_ref[:, ks].astype(BF16)], axis=0)
        scores[kv] = _dot_nt(q2, kcat)
    for kv in range(A_KV):
        lo = 2 * kv * HEAD_DIM
        ks = slice(kv * HEAD_DIM, (kv + 1) * HEAD_DIM)
        vcat = jnp.concatenate([vp_ref[:, ks].astype(BF16), vc_ref[:, ks].astype(BF16),
                                vn_ref[:, ks].astype(BF16), vx_ref[:, ks].astype(BF16)], axis=0)
        s = jnp.where(ok, scores[kv] * scale, -jnp.inf)
        p = _sink_softmax(s, _sink_column(sink_ref, kv, nq))
        o = _dot(p.astype(BF16), vcat)
        o_ref[:, lo:lo + HEAD_DIM] = o[:nq].astype(BF16)
        o_ref[:, lo + HEAD_DIM:lo + 2 * HEAD_DIM] = o[nq:].astype(BF16)


def latent_attention(qr, kr, z3, cache_k, cache_v, sink_row, *, layer):
    batch, seq, _ = z3.shape
    nb = seq // Q_BLOCK
    past = cache_k.shape[2]

    def prev(q):
        return jnp.maximum(q - 1, 0)

    def nxt(q):
        return jnp.minimum(q + 1, nb - 1)

    vcol = Z_AV // KV_W
    return pl.pallas_call(
        functools.partial(_lat_attn_kernel, seq=seq),
        grid=(batch, nb),
        in_specs=[pl.BlockSpec((None, Q_BLOCK, MIX_W), lambda b, q: (b, q, 0)),
                  pl.BlockSpec((None, Q_BLOCK, KV_W), lambda b, q: (b, prev(q), 0)),
                  pl.BlockSpec((None, Q_BLOCK, KV_W), lambda b, q: (b, q, 0)),
                  pl.BlockSpec((None, Q_BLOCK, KV_W), lambda b, q: (b, nxt(q), 0)),
                  pl.BlockSpec((None, Q_BLOCK, KV_W), lambda b, q: (b, prev(q), vcol)),
                  pl.BlockSpec((None, Q_BLOCK, KV_W), lambda b, q: (b, q, vcol)),
                  pl.BlockSpec((None, Q_BLOCK, KV_W), lambda b, q: (b, nxt(q), vcol)),
                  pl.BlockSpec((None, None, past, KV_W), lambda b, q: (b, layer, 0, 0)),
                  pl.BlockSpec((None, None, past, KV_W), lambda b, q: (b, layer, 0, 0)),
                  pl.BlockSpec((1, 128), lambda b, q: (0, 0))],
        out_specs=pl.BlockSpec((None, Q_BLOCK, MIX_W), lambda b, q: (b, q, 0)),
        out_shape=jax.ShapeDtypeStruct((batch, seq, MIX_W), BF16),
        compiler_params=_cparams("arbitrary", "arbitrary"),
        name="latent_attention",
    )(qr, kr, kr, kr, z3, z3, z3, cache_k, cache_v, sink_row)


def _gated_head_norm(total, gate, nw, act):
    y = total * lax.rsqrt(jnp.mean(total * total, axis=-1, keepdims=True) + EPS)
    return (act(gate) * (y * nw)).astype(BF16)


def _tri_mask(n, rev):
    t = lax.broadcasted_iota(jnp.int32, (n, n), 0)
    s = lax.broadcasted_iota(jnp.int32, (n, n), 1)
    return (s >= t) if rev else (s <= t)


def _chunk_index(rev, n):
    return (lambda c: n - 1 - c) if rev else (lambda c: c)


def _scan_specs(L, width, cidx):
    def zspec(col):
        return pl.BlockSpec((SCAN_SEQS, L, width), lambda b, c: (b, cidx(c), col // width))
    return zspec


def _state_spec(*dims):
    return pl.BlockSpec((SCAN_SEQS,) + dims, lambda b, c: (b,) + (0,) * len(dims))


def _mlstm_kernel(*refs, rev, d, combine):
    if combine:
        (q_ref, k_ref, v_ref, t_ref, bias_ref, c0_ref, n0_ref, m0_ref, hp_ref, g_ref, nw_ref,
         o_ref, c_ref, n_ref, m_ref) = refs
    else:
        (q_ref, k_ref, v_ref, t_ref, bias_ref, c0_ref, n0_ref, m0_ref,
         o_ref, c_ref, n_ref, m_ref) = refs
    L = SCAN_CHUNK

    @pl.when(pl.program_id(1) == 0)
    def _():
        c_ref[...] = c0_ref[...]
        n_ref[...] = n0_ref[...]
        m_ref[...] = m0_ref[...]

    tri = _tri_mask(L, rev)
    tri_b = tri.astype(BF16)
    last = 0 if rev else L - 1
    scale = HEAD_DIM ** -0.5
    gates, cum = {}, {}
    for sb in range(SCAN_SEQS):
        gates[sb] = t_ref[sb] + bias_ref[...]
        hi, mid, lo = _split3(_log_sigmoid(gates[sb]))
        cum[sb] = _dot(tri_b, hi) + _dot(tri_b, mid) + _dot(tri_b, lo)
    units = [(sb, h) for sb in range(SCAN_SEQS) for h in range(N_HEADS)]
    gates_t = {sb: gates[sb].T for sb in range(SCAN_SEQS)}
    cum_t = {sb: cum[sb].T for sb in range(SCAN_SEQS)}
    m_all = {sb: m_ref[sb] for sb in range(SCAN_SEQS)}
    n_all = {sb: n_ref[sb] for sb in range(SCAN_SEQS)}
    col, qb, ks, vb = {}, {}, {}, {}
    for u in units:
        sb, h = u
        ci, cf = d * 8 + h, d * 8 + 4 + h
        sl = slice(h * HEAD_DIM, (h + 1) * HEAD_DIM)
        i_col, b_col = gates[sb][:, ci:ci + 1], cum[sb][:, cf:cf + 1]
        m = m_all[sb][:, h:h + 1]
        bl = b_col[last:last + 1, :]
        g = bl - b_col + i_col
        m_new = jnp.maximum(bl + m, jnp.max(g, axis=0, keepdims=True))
        col[u] = dict(m=m, b=b_col, m_new=m_new, w_old=jnp.exp(bl + m - m_new), e=jnp.exp(g - m_new))
        qb[u] = q_ref[sb, :, sl].astype(BF16)
        ks[u] = k_ref[sb, :, sl] * scale
        vb[u] = v_ref[sb, :, sl].astype(BF16)
    kw = {u: col[u]["e"] * ks[u] for u in units}
    kw_t = {u: kw[u].T.astype(BF16) for u in units}
    s_raw, q_c, n_new = {}, {}, {}
    for u in units:
        sb, h = u
        c_old = c_ref[sb, h]
        s_raw[u] = _dot_nt(qb[u], ks[u].astype(BF16))
        q_c[u] = _dot(qb[u], c_old.astype(BF16))
        c_ref[sb, h] = col[u]["w_old"] * c_old + _dot(kw_t[u], vb[u])
        n_new[u] = col[u]["w_old"] * n_all[sb][h:h + 1, :] + jnp.sum(kw[u], axis=0, keepdims=True)
    qn = {u: jnp.sum(qb[u].astype(F32) * n_all[u[0]][u[1]:u[1] + 1, :].astype(BF16).astype(F32),
                     axis=1, keepdims=True) for u in units}
    wmat = {}
    for u in units:
        sb, h = u
        ci, cf = d * 8 + h, d * 8 + 4 + h
        r_row = gates_t[sb][ci:ci + 1, :] - cum_t[sb][cf:cf + 1, :]
        wmat[u] = jnp.where(tri, r_row, -jnp.inf)
    cmax = {u: jnp.max(wmat[u], axis=1, keepdims=True) for u in units}
    s = {}
    for u in units:
        col[u]["u"] = jnp.maximum(col[u]["m"], cmax[u])
        s[u] = s_raw[u] * jnp.exp(wmat[u] - col[u]["u"])
    rsum = {u: jnp.sum(s[u], axis=1, keepdims=True) for u in units}
    sv = {u: _dot(s[u].astype(BF16), vb[u]) for u in units}
    for u in units:
        sb, h = u
        sl = slice(h * HEAD_DIM, (h + 1) * HEAD_DIM)
        w_state = jnp.exp(col[u]["m"] - col[u]["u"])
        den = w_state * qn[u] + rsum[u]
        inv = 1.0 / jnp.maximum(jnp.abs(den), jnp.exp(-(col[u]["b"] + col[u]["u"])))
        hh = (w_state * q_c[u] + sv[u]) * inv
        if combine:
            o_ref[sb, :, sl] = _gated_head_norm(hh + hp_ref[sb, :, sl], g_ref[sb, :, sl], nw_ref[:, sl],
                                                jax.nn.sigmoid)
        else:
            o_ref[sb, :, sl] = hh
    for u in units:
        sb, h = u
        n_ref[sb, h:h + 1, :] = n_new[u]
        m_ref[sb, :, h:h + 1] = col[u]["m_new"]


def mlstm_direction(z3, bias_row, c0, n0, m0, *, d, prev=None, norm_w=None):
    batch, seq, _ = z3.shape
    L = SCAN_CHUNK
    nc = seq // L
    rev = d == 1
    cidx = _chunk_index(rev, nc)
    combine = prev is not None
    zspec = _scan_specs(L, MIX_W, cidx)
    state_specs = [_state_spec(N_HEADS, HEAD_DIM, HEAD_DIM), _state_spec(N_HEADS, HEAD_DIM),
                   _state_spec(1, N_HEADS)]
    in_specs = [zspec(Z_MQ), zspec(Z_MK), zspec(Z_MV),
                pl.BlockSpec((SCAN_SEQS, L, 128), lambda b, c: (b, cidx(c), Z_TAIL // 128)),
                pl.BlockSpec((1, 128), lambda b, c: (0, 0))] + state_specs
    args = [z3, z3, z3, z3, bias_row, c0, n0, m0]
    if combine:
        in_specs += [zspec(0), zspec(Z_MO), pl.BlockSpec((1, MIX_W), lambda b, c: (0, 0))]
        args += [prev, z3, norm_w]
    return pl.pallas_call(
        functools.partial(_mlstm_kernel, rev=rev, d=d, combine=combine),
        grid=(batch // SCAN_SEQS, nc),
        in_specs=in_specs,
        out_specs=[zspec(0)] + state_specs,
        out_shape=[jax.ShapeDtypeStruct((batch, seq, MIX_W), BF16 if combine else F32),
                   jax.ShapeDtypeStruct((batch, N_HEADS, HEAD_DIM, HEAD_DIM), F32),
                   jax.ShapeDtypeStruct((batch, N_HEADS, HEAD_DIM), F32),
                   jax.ShapeDtypeStruct((batch, 1, N_HEADS), F32)],
        compiler_params=_cparams("arbitrary", "arbitrary"),
        name="mlstm_scan",
    )(*args)


def _retention_kernel(*refs, rev, combine):
    if combine:
        q_ref, k_ref, v_ref, dec_ref, s0_ref, hp_ref, g_ref, nw_ref, o_ref, s_ref = refs
    else:
        q_ref, k_ref, v_ref, dec_ref, s0_ref, o_ref, s_ref = refs
    L = SCAN_CHUNK

    @pl.when(pl.program_id(1) == 0)
    def _():
        s_ref[...] = s0_ref[...]

    tri = _tri_mask(L, rev)
    t_i = lax.broadcasted_iota(jnp.int32, (L, L), 0)
    s_i = lax.broadcasted_iota(jnp.int32, (L, L), 1)
    dist = jnp.abs(t_i - s_i).astype(F32)
    pos = lax.broadcasted_iota(jnp.int32, (L, 1), 0).astype(F32)
    xi_pow = (L - pos) if rev else (pos + 1.0)
    zeta_pow = pos if rev else (L - 1.0 - pos)
    log_gamma = _log_sigmoid(dec_ref[...])
    scale = HEAD_DIM ** -0.5
    units = [(sb, h) for h in range(N_HEADS) for sb in range(SCAN_SEQS)]
    stage1 = {}
    for sb, h in units:
        sl = slice(h * HEAD_DIM, (h + 1) * HEAD_DIM)
        lg = log_gamma[:, h:h + 1]
        qb = q_ref[sb, :, sl].astype(BF16)
        ks = k_ref[sb, :, sl] * scale
        vb = v_ref[sb, :, sl].astype(BF16)
        s_old = s_ref[sb, h]
        att = _dot_nt(qb, ks.astype(BF16))
        inter = _dot(qb, s_old.astype(BF16))
        outer = _dot((ks * jnp.exp(zeta_pow * lg)).T.astype(BF16), vb)
        s_ref[sb, h] = jnp.exp(L * lg) * s_old + outer
        stage1[sb, h] = (att, inter, vb)
    decay, xi = {}, {}
    for h in range(N_HEADS):
        lg = log_gamma[:, h:h + 1]
        decay[h] = jnp.where(tri, jnp.exp(dist * lg), 0.0)
        xi[h] = jnp.exp(xi_pow * lg)
    for sb, h in units:
        sl = slice(h * HEAD_DIM, (h + 1) * HEAD_DIM)
        att, inter, vb = stage1[sb, h]
        y = _dot((att * decay[h]).astype(BF16), vb) + xi[h] * inter
        if combine:
            o_ref[sb, :, sl] = _gated_head_norm(y + hp_ref[sb, :, sl], g_ref[sb, :, sl], nw_ref[:, sl],
                                                jax.nn.silu)
        else:
            o_ref[sb, :, sl] = y


def retention_direction(z3, decay_row, s0, *, d, prev=None, norm_w=None):
    batch, seq, _ = z3.shape
    L = SCAN_CHUNK
    nc = seq // L
    rev = d == 1
    cidx = _chunk_index(rev, nc)
    combine = prev is not None
    zspec = _scan_specs(L, MIX_W, cidx)
    state_spec = _state_spec(N_HEADS, HEAD_DIM, HEAD_DIM)
    in_specs = [zspec(Z_RQ), zspec(Z_RK), zspec(Z_RV), pl.BlockSpec((1, 128), lambda b, c: (0, 0)), state_spec]
    args = [z3, z3, z3, decay_row, s0]
    if combine:
        in_specs += [zspec(0), zspec(Z_RG), pl.BlockSpec((1, MIX_W), lambda b, c: (0, 0))]
        args += [prev, z3, norm_w]
    return pl.pallas_call(
        functools.partial(_retention_kernel, rev=rev, combine=combine),
        grid=(batch // SCAN_SEQS, nc),
        in_specs=in_specs,
        out_specs=[zspec(0), state_spec],
        out_shape=[jax.ShapeDtypeStruct((batch, seq, MIX_W), BF16 if combine else F32),
                   jax.ShapeDtypeStruct((batch, N_HEADS, HEAD_DIM, HEAD_DIM), F32)],
        compiler_params=_cparams("arbitrary", "arbitrary"),
        name="retention_scan",
    )(*args)


GLA_LEVELS = (32, 16, 8, 4, 2, 1)
GLA_LANES = N_HEADS * G_DK


def _gla_constants(rev):
    L = GLA_CHUNK
    t = np.arange(L)[:, None]
    r = np.arange(L)[None, :]
    mats = [(r <= t).astype(np.float32)]
    pair = [np.eye(L, dtype=np.float32)]
    for w in GLA_LEVELS:
        blk_t, blk_r = t // (2 * w), r // (2 * w)
        ref_t = blk_t * 2 * w + w - 1
        upper_t = (t % (2 * w)) >= w
        mats.append((upper_t & (r > ref_t) & (r <= t)).astype(np.float32))
        mats.append(((~upper_t) & (r > t) & (r <= ref_t)).astype(np.float32))
        lower_r = (r % (2 * w)) < w
        pair.append((upper_t & lower_r & (blk_t == blk_r)).astype(np.float32))
    if rev:
        mats = [m[::-1, ::-1] for m in mats]
        pair = [p[::-1, ::-1] for p in pair]
    big = np.concatenate(mats, axis=0)
    big3 = np.concatenate([big, big, big], axis=1)
    pair = np.stack([np.tile(p, (1, N_HEADS)) for p in pair])
    return jnp.asarray(big3, BF16), jnp.asarray(pair, F32)


def _gla_block_masks():
    L = GLA_CHUNK
    rows = np.arange(N_HEADS * L)[:, None] // L
    kmask = (rows == np.arange(GLA_LANES)[None, :] // G_DK).astype(np.float32)
    vmask = (rows == np.arange(MIX_W)[None, :] // HEAD_DIM).astype(np.float32)
    srow = np.arange(GLA_LANES)[:, None] // G_DK
    smask = (srow == np.arange(MIX_W)[None, :] // HEAD_DIM).astype(np.float32)
    return jnp.asarray(kmask, F32), jnp.asarray(vmask, F32), jnp.asarray(smask, F32)


def _gla_kernel(*refs, rev, combine):
    if combine:
        (q_ref, k_ref, v_ref, t_ref, w2_ref, gb_ref, big_ref, pair_ref, km_ref, vm_ref, sm_ref, s0_ref,
         hp_ref, g_ref, nw_ref, o_ref, so_ref, s_ref) = refs
    else:
        (q_ref, k_ref, v_ref, t_ref, w2_ref, gb_ref, big_ref, pair_ref, km_ref, vm_ref, sm_ref, s0_ref,
         o_ref, so_ref, s_ref) = refs
    L = GLA_CHUNK
    c = pl.program_id(1)

    @pl.when(c == 0)
    def _():
        s_ref[...] = jnp.zeros_like(s_ref)
        for sb in range(SCAN_SEQS):
            for h in range(N_HEADS):
                s_ref[sb, h * G_DK:(h + 1) * G_DK, h * HEAD_DIM:(h + 1) * HEAD_DIM] = s0_ref[sb, h]

    kmask = km_ref[...]
    last = 0 if rev else L - 1

    def stacked(x):
        return (jnp.concatenate([x] * N_HEADS, axis=0) * kmask).astype(BF16)

    seqs = range(SCAN_SEQS)
    logit = {sb: _dot(t_ref[sb].astype(BF16), w2_ref[...]) + gb_ref[...] for sb in seqs}
    e = {}
    for sb in seqs:
        la = _log_sigmoid(logit[sb]) * (1.0 / GLA_TAU)
        e[sb] = _dot(big_ref[...], jnp.concatenate(_split3(la), axis=0))
    pad = jnp.zeros((128 - L, GLA_LANES), F32)
    stage = {}
    for sb in seqs:
        cum = e[sb][0:L]
        q = q_ref[sb] * (G_DK ** -0.5)
        k = k_ref[sb]
        v = v_ref[sb]
        s_old = s_ref[sb]
        parts = [_dot_nt(q.astype(BF16), stacked(k))]
        for i in range(len(GLA_LEVELS)):
            eq = e[sb][(2 * i + 1) * L:(2 * i + 2) * L]
            ek = e[sb][(2 * i + 2) * L:(2 * i + 3) * L]
            parts.append(_dot_nt((q * jnp.exp(eq)).astype(BF16), stacked(k * jnp.exp(ek))))
        inter = _dot((q * jnp.exp(cum)).astype(BF16), s_old.astype(BF16))
        last_row = cum[last:last + 1, :]
        kd_t = jnp.concatenate([k * jnp.exp(last_row - cum), pad], axis=0).T
        cum_t = jnp.concatenate([cum, pad], axis=0).T
        v_pad = jnp.concatenate([v, jnp.zeros((128 - L, MIX_W), F32)], axis=0).astype(BF16)
        s_ref[sb] = jnp.exp(cum_t[:, last:last + 1]) * s_old + sm_ref[...] * _dot(kd_t.astype(BF16), v_pad)
        stage[sb] = (parts, inter, v)
    for sb in seqs:
        parts, inter, v = stage[sb]
        att = pair_ref[0] * parts[0]
        for i in range(len(GLA_LEVELS)):
            att = att + pair_ref[i + 1] * parts[i + 1]
        vblk = (jnp.concatenate([v] * N_HEADS, axis=0) * vm_ref[...]).astype(BF16)
        y = _dot(att.astype(BF16), vblk) + inter
        if combine:
            for h in range(N_HEADS):
                sl = slice(h * HEAD_DIM, (h + 1) * HEAD_DIM)
                o_ref[sb, :, sl] = _gated_head_norm(y[:, sl] + hp_ref[sb, :, sl], g_ref[sb, :, sl],
                                                    nw_ref[:, sl], jax.nn.silu)
        else:
            o_ref[sb] = y

    @pl.when(c == pl.num_programs(1) - 1)
    def _():
        for sb in range(SCAN_SEQS):
            for h in range(N_HEADS):
                so_ref[sb, h] = s_ref[sb, h * G_DK:(h + 1) * G_DK, h * HEAD_DIM:(h + 1) * HEAD_DIM]


def gla_direction(z3, w2pad, gbias, s0, *, d, prev=None, norm_w=None):
    batch, seq, _ = z3.shape
    L = GLA_CHUNK
    nc = seq // L
    rev = d == 1
    cidx = _chunk_index(rev, nc)
    combine = prev is not None
    big3, pair = _gla_constants(rev)
    kmask, vmask, smask = _gla_block_masks()

    def const(a):
        return pl.BlockSpec(a.shape, lambda b, c: (0,) * a.ndim)

    zq = _scan_specs(L, GLA_LANES, cidx)
    zv = _scan_specs(L, MIX_W, cidx)
    state_spec = _state_spec(N_HEADS, G_DK, HEAD_DIM)
    in_specs = [zq(Z_GQ), zq(Z_GK), zv(Z_GV),
                pl.BlockSpec((SCAN_SEQS, L, 128), lambda b, c: (b, cidx(c), Z_TAIL // 128)),
                const(w2pad), const(gbias), const(big3), const(pair), const(kmask), const(vmask), const(smask),
                state_spec]
    args = [z3, z3, z3, z3, w2pad, gbias, big3, pair, kmask, vmask, smask, s0]
    if combine:
        in_specs += [zv(0), zv(Z_GG), pl.BlockSpec((1, MIX_W), lambda b, c: (0, 0))]
        args += [prev, z3, norm_w]
    return pl.pallas_call(
        functools.partial(_gla_kernel, rev=rev, combine=combine),
        grid=(batch // SCAN_SEQS, nc),
        in_specs=in_specs,
        out_specs=[zv(0), state_spec],
        out_shape=[jax.ShapeDtypeStruct((batch, seq, MIX_W), BF16 if combine else F32),
                   jax.ShapeDtypeStruct((batch, N_HEADS, G_DK, HEAD_DIM), F32)],
        scratch_shapes=[pltpu.VMEM((SCAN_SEQS, GLA_LANES, MIX_W), F32)],
        compiler_params=_cparams("arbitrary", "arbitrary"),
        name="gla_scan",
    )(*args)


def _pad_row(v, width=128):
    v = v.reshape(1, -1).astype(F32)
    return jnp.pad(v, ((0, 0), (0, width - v.shape[1])))


def _permute_w_in(w_in_l):
    sizes = (512, 256, 256, 512, 512, 512, 512, 16, 512, 512, 512, 512, 256, 256, 512, 512, 32)
    offs = np.concatenate([[0], np.cumsum(sizes)])
    pieces = [w_in_l[:, offs[i]:offs[i + 1]] for i in range(len(sizes))]
    order = [0, 1, 2, 3, 4, 5, 6, 8, 9, 10, 11, 12, 13, 14, 15, 7, 16]
    w = jnp.concatenate([pieces[i] for i in order], axis=1)
    return jnp.pad(w, ((0, 0), (0, Z_COLS - w.shape[1]))).astype(BF16)


def _prepare_layer(p, l):
    d = p["w_in"].shape[1]
    w2pad = []
    for direction in range(2):
        lo = TAIL_GLR + direction * G_RANK
        w2pad.append(jnp.zeros((128, GLA_LANES), F32).at[lo:lo + G_RANK].set(p["gla_w2"][l, direction]).astype(BF16))
    return {
        "w_in": _permute_w_in(p["w_in"][l]),
        "w_mgate": p["w_mgate"][l].astype(BF16), "w_br": p["w_br"][l].astype(BF16),
        "w_out": p["w_out"][l].astype(BF16), "ffn_w_gu": p["ffn_w_gu"][l].astype(BF16),
        "ffn_w_down": p["ffn_w_down"][l].astype(BF16),
        "norm1_w": p["norm1_w"][l].reshape(1, d), "norm2_w": p["norm2_w"][l].reshape(1, d),
        "sink_row": _pad_row(p["attn_sink"][l]), "mlstm_bias": _pad_row(p["mlstm_if_b"][l]),
        "mlstm_norm_w": p["mlstm_norm_w"][l].reshape(1, MIX_W),
        "ret_decay": [_pad_row(p["ret_decay"][l, 0]), _pad_row(p["ret_decay"][l, 1])],
        "ret_norm_w": p["ret_norm_w"][l].reshape(1, MIX_W),
        "gla_w2": w2pad, "gla_b": [p["gla_b"][l, 0].reshape(1, GLA_LANES), p["gla_b"][l, 1].reshape(1, GLA_LANES)],
        "gla_norm_w": p["gla_norm_w"][l].reshape(1, MIX_W),
    }


def _mixers(z3, w, l, cache):
    batch, seq, _ = z3.shape
    is_ctx = cache is None
    if is_ctx:
        ya = context_attention(z3, w["sink_row"])
    else:
        cos, sin = rope_tables(seq)
        qr, kr = rope_qk(z3, cos, sin, tb=min(seq, 512))
        ck = cache["attn_k"].reshape(cache["attn_k"].shape[:3] + (KV_W,))
        cv = cache["attn_v"].reshape(cache["attn_v"].shape[:3] + (KV_W,))
        ya = latent_attention(qr, kr, z3, ck, cv, w["sink_row"], layer=l)

    def zeros(*dims):
        return jnp.zeros((batch,) + dims, F32)

    res = {}
    for d in (1, 0):
        if is_ctx:
            init = (zeros(N_HEADS, HEAD_DIM, HEAD_DIM), zeros(N_HEADS, HEAD_DIM), zeros(1, N_HEADS))
        else:
            init = (cache["mlstm_C"][:, l, d], cache["mlstm_n"][:, l, d],
                    cache["mlstm_m"][:, l, d].reshape(batch, 1, N_HEADS))
        res[d] = mlstm_direction(z3, w["mlstm_bias"], *init, d=d, prev=res[1][0] if d == 0 else None,
                                 norm_w=w["mlstm_norm_w"] if d == 0 else None)
    ym = res[0][0]
    states = {"mlstm_C": jnp.stack([res[0][1], res[1][1]], axis=1),
              "mlstm_n": jnp.stack([res[0][2], res[1][2]], axis=1),
              "mlstm_m": jnp.stack([res[0][3][:, 0], res[1][3][:, 0]], axis=1)}

    res = {}
    for d in (1, 0):
        s0 = zeros(N_HEADS, HEAD_DIM, HEAD_DIM) if is_ctx else cache["ret_S"][:, l, d]
        res[d] = retention_direction(z3, w["ret_decay"][d], s0, d=d, prev=res[1][0] if d == 0 else None,
                                     norm_w=w["ret_norm_w"] if d == 0 else None)
    yr = res[0][0]
    states["ret_S"] = jnp.stack([res[0][1], res[1][1]], axis=1)

    res = {}
    for d in (1, 0):
        s0 = zeros(N_HEADS, G_DK, HEAD_DIM) if is_ctx else cache["gla_S"][:, l, d]
        res[d] = gla_direction(z3, w["gla_w2"][d], w["gla_b"][d], s0, d=d, prev=res[1][0] if d == 0 else None,
                               norm_w=w["gla_norm_w"] if d == 0 else None)
    yg = res[0][0]
    states["gla_S"] = jnp.stack([res[0][1], res[1][1]], axis=1)

    rows = batch * seq
    ys = [y.reshape(rows, MIX_W) for y in (ya, ym, yr, yg)]
    states["attn_k"] = z3[:, :, Z_AK:Z_AK + KV_W].reshape(batch, seq, A_KV, HEAD_DIM)
    states["attn_v"] = z3[:, :, Z_AV:Z_AV + KV_W].reshape(batch, seq, A_KV, HEAD_DIM)
    return ys, states


def _layer(x, dims, mod_l, w, l, cache, *, grp, tm):
    batch, seq = dims
    d = x.shape[1]
    sh1, sc1, g1, sh2, sc2, g2 = (mod_l[:, i * d:(i + 1) * d] for i in range(6))
    z, h = in_projection(x, sh1, sc1, w["norm1_w"], w["w_in"], grp=grp, tm=tm)
    ys, states = _mixers(z.reshape(batch, seq, Z_COLS), w, l, cache)
    merged = merge_branches(h, ys, w["w_mgate"], w["w_br"], tm=tm, tn=256)
    x = residual_projection(merged, w["w_out"], x, g1, grp=grp, tm=tm, tn=512)
    act = ffn_up(x, sh2, sc2, w["norm2_w"], w["ffn_w_gu"], grp=grp, tm=tm, tn=512)
    x = residual_projection(act, w["ffn_w_down"], x, g2, grp=grp, tm=tm, tn=256)
    return x, states


def _forward(x_prompt, x_sample, cache, c, c_ctx, p, final_norm_w, *, tm):
    cb, ct, d = x_prompt.shape
    lb, lt, _ = x_sample.shape
    assert (cb * ct) % tm == 0 and lt % tm == 0 and lb + 1 <= MOD_ROWS
    assert cb % SCAN_SEQS == 0 and lb % SCAN_SEQS == 0
    depth = p["w_ada"].shape[0]
    cond = jnp.concatenate([c_ctx[None, :], c, jnp.zeros((MOD_ROWS - 1 - lb, d), F32)], axis=0)
    mod = ada_modulation(cond, p["w_ada"], p["b_ada"])
    xc = x_prompt.reshape(cb * ct, d)
    xl = x_sample.reshape(lb * lt, d)
    ctx_states = []
    for l in range(depth):
        w = _prepare_layer(p, l)
        xc, st = _layer(xc, (cb, ct), mod[l], w, l, None, grp=(0, cb * ct), tm=tm)
        xl, _ = _layer(xl, (lb, lt), mod[l], w, l, cache, grp=(1, lt), tm=tm)
        ctx_states.append(st)
    fw = final_norm_w.reshape(1, d)
    y_prompt = final_norm(xc, fw, tm=tm).reshape(cb, ct, d)
    y_sample = final_norm(xl, fw, tm=tm).reshape(lb, lt, d)

    def stack(name):
        return jnp.stack([s[name] for s in ctx_states], axis=1)

    return (y_prompt, y_sample, stack("attn_k"), stack("attn_v"), stack("mlstm_C"), stack("mlstm_n"),
            stack("mlstm_m"), stack("ret_S"), stack("gla_S"))


def kernel(x_prompt, x_sample, cache_attn_k, cache_attn_v, state_mlstm_C, state_mlstm_n, state_mlstm_m, state_ret_S, state_gla_S, c, c_ctx, w_ada, b_ada, norm1_w, norm2_w, w_in, attn_sink, mlstm_if_b, mlstm_norm_w, ret_decay, ret_norm_w, gla_w2, gla_b, gla_norm_w, w_br, w_mgate, w_out, ffn_w_gu, ffn_w_down, final_norm_w):
    p = {"w_ada": w_ada, "b_ada": b_ada, "norm1_w": norm1_w, "norm2_w": norm2_w, "w_in": w_in,
         "attn_sink": attn_sink, "mlstm_if_b": mlstm_if_b, "mlstm_norm_w": mlstm_norm_w,
         "ret_decay": ret_decay, "ret_norm_w": ret_norm_w, "gla_w2": gla_w2, "gla_b": gla_b,
         "gla_norm_w": gla_norm_w, "w_br": w_br, "w_mgate": w_mgate, "w_out": w_out,
         "ffn_w_gu": ffn_w_gu, "ffn_w_down": ffn_w_down}
    cache = {"attn_k": cache_attn_k, "attn_v": cache_attn_v, "mlstm_C": state_mlstm_C,
             "mlstm_n": state_mlstm_n, "mlstm_m": state_mlstm_m, "ret_S": state_ret_S, "gla_S": state_gla_S}
    return _forward(x_prompt, x_sample, cache, c, c_ctx, p, final_norm_w, tm=1024)
```

```python
import functools

import numpy as np
import jax
import jax.numpy as jnp
from jax import lax
from jax.experimental import pallas as pl
from jax.experimental.pallas import tpu as pltpu

F32 = jnp.float32
BF16 = jnp.bfloat16

HEAD_DIM = 128
N_HEADS = 4
A_KV = 2
GRID_W = 64
WINDOW = 128
Q_BLOCK = 128
G_DK = 64
G_RANK = 16
GLA_TAU = 16.0
ROPE_BASE = 10000.0
EPS = 1e-6
MOD_ROWS = 16
SCAN_CHUNK = 128
GLA_CHUNK = 64
SCAN_SEQS = 4
ATT_QBLOCKS = 2
MIX_W = N_HEADS * HEAD_DIM
KV_W = A_KV * HEAD_DIM

Z_AQ = 0
Z_MQ, Z_MK, Z_MV = 512, 1024, 1536
Z_RQ, Z_RK, Z_RV = 2048, 2560, 3072
Z_GQ, Z_GK, Z_GV = 3584, 3840, 4096
ZB_COLS = 4608
Z_AK, Z_AV = 0, 256
Z_MO, Z_RG, Z_GG = 512, 1024, 1536
Z_TAIL = 2048
ZF_COLS = 2304
Z_TN = 768
TAIL_GLR = 16

VMEM_LIMIT = 56 * 2 ** 20


def _cparams(*sem):
    return pltpu.CompilerParams(dimension_semantics=sem, vmem_limit_bytes=VMEM_LIMIT)


def _log_sigmoid(x):
    return jnp.minimum(x, 0.0) - jnp.log1p(jnp.exp(-jnp.abs(x)))


def _split3(x):
    hi = x.astype(BF16)
    r = x - hi.astype(F32)
    mid = r.astype(BF16)
    lo = (r - mid.astype(F32)).astype(BF16)
    return hi, mid, lo


def _dot(a, b):
    return jnp.dot(a, b, preferred_element_type=F32)


def _dot_nt(a, b):
    return lax.dot_general(a, b, (((1,), (1,)), ((), ())), preferred_element_type=F32)


def _ada_kernel(cond_ref, w_ref, b_ref, o_ref):
    s = jax.nn.silu(cond_ref[...]).astype(BF16)
    o_ref[0] = _dot(s, w_ref[0].astype(BF16)) + b_ref[0]


def ada_modulation(cond, w_ada, b_ada):
    depth, d, n = w_ada.shape
    tn = 1024
    return pl.pallas_call(
        _ada_kernel,
        grid=(depth, n // tn),
        in_specs=[pl.BlockSpec((MOD_ROWS, d), lambda l, j: (0, 0)),
                  pl.BlockSpec((1, d, tn), lambda l, j: (l, 0, j)),
                  pl.BlockSpec((1, 1, tn), lambda l, j: (l, 0, j))],
        out_specs=pl.BlockSpec((1, MOD_ROWS, tn), lambda l, j: (l, 0, j)),
        out_shape=jax.ShapeDtypeStruct((depth, MOD_ROWS, n), F32),
        compiler_params=_cparams("arbitrary", "arbitrary"),
        name="ada_modulation",
    )(cond, w_ada, b_ada.reshape(depth, 1, n))


def _mod_row(grp, tm):
    return grp[0] + (pl.program_id(0) * tm) // grp[1]


def _modulated_norm(x, nw, sh, sc):
    y = x * lax.rsqrt(jnp.mean(x * x, axis=-1, keepdims=True) + EPS)
    return (y * nw) * (1.0 + sc) + sh


def _in_proj_kernel(x_ref, sh_ref, sc_ref, nw_ref, w_ref, zb_ref, zf_ref, h_ref, *, tm, grp, nb):
    j = pl.program_id(1)

    @pl.when(j == 0)
    def _():
        g = _mod_row(grp, tm)
        h = _modulated_norm(x_ref[...], nw_ref[...], sh_ref[pl.ds(g, 1), :], sc_ref[pl.ds(g, 1), :])
        h_ref[...] = h.astype(BF16)

    z = _dot(h_ref[...], w_ref[...])

    @pl.when(j < nb)
    def _():
        zb_ref[...] = z.astype(BF16)

    @pl.when(j >= nb)
    def _():
        zf_ref[...] = z


def in_projection(x, sh, sc, nw, w, *, grp, tm):
    rows, d = x.shape
    nb, nf = ZB_COLS // Z_TN, ZF_COLS // Z_TN
    return pl.pallas_call(
        functools.partial(_in_proj_kernel, tm=tm, grp=grp, nb=nb),
        grid=(rows // tm, nb + nf),
        in_specs=[pl.BlockSpec((tm, d), lambda i, j: (i, 0)),
                  pl.BlockSpec((MOD_ROWS, d), lambda i, j: (0, 0)),
                  pl.BlockSpec((MOD_ROWS, d), lambda i, j: (0, 0)),
                  pl.BlockSpec((1, d), lambda i, j: (0, 0)),
                  pl.BlockSpec((d, Z_TN), lambda i, j: (0, j))],
        out_specs=[pl.BlockSpec((tm, Z_TN), lambda i, j: (i, jnp.minimum(j, nb - 1))),
                   pl.BlockSpec((tm, Z_TN), lambda i, j: (i, jnp.maximum(j - nb, 0))),
                   pl.BlockSpec((tm, d), lambda i, j: (i, 0))],
        out_shape=[jax.ShapeDtypeStruct((rows, ZB_COLS), BF16), jax.ShapeDtypeStruct((rows, ZF_COLS), F32),
                   jax.ShapeDtypeStruct((rows, d), BF16)],
        compiler_params=_cparams("arbitrary", "arbitrary"),
        name="in_projection",
    )(x, sh, sc, nw, w)


def _merge_kernel(h_ref, ya_ref, ym_ref, yr_ref, yg_ref, wg_ref, wb_ref, o_ref):
    h = h_ref[...]
    acc = None
    for b, y_ref in enumerate((ya_ref, ym_ref, yr_ref, yg_ref)):
        gate = jax.nn.sigmoid(_dot(h, wg_ref[b]))
        term = gate * _dot(y_ref[...], wb_ref[b])
        acc = term if acc is None else acc + term
    o_ref[...] = acc.astype(BF16)


def merge_branches(h, ys, w_mgate, w_br, *, tm, tn):
    rows, d = h.shape
    nb, mw, _ = w_br.shape
    return pl.pallas_call(
        _merge_kernel,
        grid=(rows // tm, d // tn),
        in_specs=[pl.BlockSpec((tm, d), lambda i, j: (i, 0))]
        + [pl.BlockSpec((tm, mw), lambda i, j: (i, 0))] * nb
        + [pl.BlockSpec((nb, d, tn), lambda i, j: (0, 0, j)),
           pl.BlockSpec((nb, mw, tn), lambda i, j: (0, 0, j))],
        out_specs=pl.BlockSpec((tm, tn), lambda i, j: (i, j)),
        out_shape=jax.ShapeDtypeStruct((rows, d), BF16),
        compiler_params=_cparams("arbitrary", "arbitrary"),
        name="merge_branches",
    )(h, *ys, w_mgate, w_br)


def _residual_proj_kernel(a_ref, w_ref, x_ref, g_ref, o_ref, *, tm, grp):
    g = _mod_row(grp, tm)
    o_ref[...] = x_ref[...] + g_ref[pl.ds(g, 1), :] * _dot(a_ref[...], w_ref[...])


def residual_projection(a, w, x, gate, *, grp, tm, tn):
    rows, k = a.shape
    n = w.shape[1]
    return pl.pallas_call(
        functools.partial(_residual_proj_kernel, tm=tm, grp=grp),
        grid=(rows // tm, n // tn),
        in_specs=[pl.BlockSpec((tm, k), lambda i, j: (i, 0)),
                  pl.BlockSpec((k, tn), lambda i, j: (0, j)),
                  pl.BlockSpec((tm, tn), lambda i, j: (i, j)),
                  pl.BlockSpec((MOD_ROWS, tn), lambda i, j: (0, j))],
        out_specs=pl.BlockSpec((tm, tn), lambda i, j: (i, j)),
        out_shape=jax.ShapeDtypeStruct((rows, n), F32),
        compiler_params=_cparams("arbitrary", "arbitrary"),
        name="residual_projection",
    )(a, w, x, gate)


def _out_proj_norm_kernel(a_ref, w_ref, x_ref, g_ref, sh_ref, sc_ref, nw_ref, o_ref, h_ref, *, tm, grp):
    g = _mod_row(grp, tm)
    x = x_ref[...] + g_ref[pl.ds(g, 1), :] * _dot(a_ref[...], w_ref[...])
    o_ref[...] = x
    h_ref[...] = _modulated_norm(x, nw_ref[...], sh_ref[pl.ds(g, 1), :], sc_ref[pl.ds(g, 1), :]).astype(BF16)


def out_projection_norm(a, w, x, gate, sh, sc, nw, *, grp, tm):
    rows, k = a.shape
    d = w.shape[1]
    full = lambda i: (0, 0)
    return pl.pallas_call(
        functools.partial(_out_proj_norm_kernel, tm=tm, grp=grp),
        grid=(rows // tm,),
        in_specs=[pl.BlockSpec((tm, k), lambda i: (i, 0)),
                  pl.BlockSpec((k, d), full),
                  pl.BlockSpec((tm, d), lambda i: (i, 0)),
                  pl.BlockSpec((MOD_ROWS, d), full),
                  pl.BlockSpec((MOD_ROWS, d), full),
                  pl.BlockSpec((MOD_ROWS, d), full),
                  pl.BlockSpec((1, d), full)],
        out_specs=[pl.BlockSpec((tm, d), lambda i: (i, 0)), pl.BlockSpec((tm, d), lambda i: (i, 0))],
        out_shape=[jax.ShapeDtypeStruct((rows, d), F32), jax.ShapeDtypeStruct((rows, d), BF16)],
        compiler_params=_cparams("arbitrary"),
        name="out_projection_norm",
    )(a, w, x, gate, sh, sc, nw)


def _ffn_up_kernel(h_ref, wg_ref, wu_ref, o_ref):
    h = h_ref[...]
    o_ref[...] = (jax.nn.silu(_dot(h, wg_ref[...])) * _dot(h, wu_ref[...])).astype(BF16)


def ffn_up(h, w_gu, *, tm, tn):
    rows, d = h.shape
    hid = w_gu.shape[1] // 2
    nj = hid // tn
    return pl.pallas_call(
        _ffn_up_kernel,
        grid=(rows // tm, nj),
        in_specs=[pl.BlockSpec((tm, d), lambda i, j: (i, 0)),
                  pl.BlockSpec((d, tn), lambda i, j: (0, j)),
                  pl.BlockSpec((d, tn), lambda i, j: (0, j + nj))],
        out_specs=pl.BlockSpec((tm, tn), lambda i, j: (i, j)),
        out_shape=jax.ShapeDtypeStruct((rows, hid), BF16),
        compiler_params=_cparams("arbitrary", "arbitrary"),
        name="ffn_up",
    )(h, w_gu, w_gu)


def _final_norm_kernel(x_ref, w_ref, o_ref):
    x = x_ref[...]
    o_ref[...] = x * lax.rsqrt(jnp.mean(x * x, axis=-1, keepdims=True) + EPS) * w_ref[...]


def final_norm(x, w, *, tm):
    rows, d = x.shape
    return pl.pallas_call(
        _final_norm_kernel,
        grid=(rows // tm,),
        in_specs=[pl.BlockSpec((tm, d), lambda i: (i, 0)), pl.BlockSpec((1, d), lambda i: (0, 0))],
        out_specs=pl.BlockSpec((tm, d), lambda i: (i, 0)),
        out_shape=jax.ShapeDtypeStruct((rows, d), F32),
        compiler_params=_cparams("arbitrary"),
        name="final_norm",
    )(x, w)


def _sink_softmax(s, sink_col):
    m = jnp.maximum(jnp.max(s, axis=-1, keepdims=True), sink_col)
    p = jnp.exp(s - m)
    return p * (1.0 / (jnp.sum(p, axis=-1, keepdims=True) + jnp.exp(sink_col - m)))


def _sink_column(sink_ref, kv, rows):
    half = lax.broadcasted_iota(jnp.int32, (2 * rows, 1), 0) < rows
    return jnp.where(half, sink_ref[:, 2 * kv:2 * kv + 1], sink_ref[:, 2 * kv + 1:2 * kv + 2])


def _ctx_attn_kernel(q_ref, k_ref, v_ref, sink_ref, o_ref):
    t = q_ref.shape[0]
    scale = HEAD_DIM ** -0.5
    scores = {}
    for kv in range(A_KV):
        lo = 2 * kv * HEAD_DIM
        q2 = jnp.concatenate([q_ref[:, lo:lo + HEAD_DIM], q_ref[:, lo + HEAD_DIM:lo + 2 * HEAD_DIM]], axis=0)
        k = k_ref[:, kv * HEAD_DIM:(kv + 1) * HEAD_DIM].astype(BF16)
        scores[kv] = _dot_nt(q2.astype(BF16), k) * scale
    for kv in range(A_KV):
        lo = 2 * kv * HEAD_DIM
        v = v_ref[:, kv * HEAD_DIM:(kv + 1) * HEAD_DIM].astype(BF16)
        p = _sink_softmax(scores[kv], _sink_column(sink_ref, kv, t))
        o = _dot(p.astype(BF16), v)
        o_ref[:, lo:lo + HEAD_DIM] = o[:t].astype(BF16)
        o_ref[:, lo + HEAD_DIM:lo + 2 * HEAD_DIM] = o[t:].astype(BF16)


def context_attention(zb, zf, sink_row):
    batch, seq, _ = zb.shape
    return pl.pallas_call(
        _ctx_attn_kernel,
        grid=(batch,),
        in_specs=[pl.BlockSpec((None, seq, MIX_W), lambda b: (b, 0, Z_AQ // MIX_W)),
                  pl.BlockSpec((None, seq, KV_W), lambda b: (b, 0, Z_AK // KV_W)),
                  pl.BlockSpec((None, seq, KV_W), lambda b: (b, 0, Z_AV // KV_W)),
                  pl.BlockSpec((1, 128), lambda b: (0, 0))],
        out_specs=pl.BlockSpec((None, seq, MIX_W), lambda b: (b, 0, 0)),
        out_shape=jax.ShapeDtypeStruct((batch, seq, MIX_W), BF16),
        compiler_params=_cparams("arbitrary"),
        name="context_attention",
    )(zb, zf, zf, sink_row)


def _rope(x, cos, sin_signed):
    lane = lax.broadcasted_iota(jnp.int32, x.shape, 1)
    swapped = jnp.where(lane % 64 < 32, pltpu.roll(x, 96, 1), pltpu.roll(x, 32, 1))
    return x * cos + swapped * sin_signed


def _rope_kernel(q_ref, k_ref, cos_ref, sin_ref, qo_ref, ko_ref):
    cos, sin = cos_ref[...], sin_ref[...]
    for h in range(N_HEADS):
        sl = slice(h * HEAD_DIM, (h + 1) * HEAD_DIM)
        qo_ref[:, sl] = _rope(q_ref[:, sl].astype(F32), cos, sin).astype(BF16)
    for h in range(A_KV):
        sl = slice(h * HEAD_DIM, (h + 1) * HEAD_DIM)
        ko_ref[:, sl] = _rope(k_ref[:, sl], cos, sin).astype(BF16)


def rope_tables(seq):
    pos = np.arange(seq)
    quarter = HEAD_DIM // 4
    inv = jnp.asarray(ROPE_BASE, F32) ** (-jnp.arange(quarter, dtype=F32) / quarter)
    row = jnp.asarray(pos // GRID_W, F32)[:, None] * inv[None, :]
    col = jnp.asarray(pos % GRID_W, F32)[:, None] * inv[None, :]
    cos = jnp.concatenate([jnp.cos(row), jnp.cos(row), jnp.cos(col), jnp.cos(col)], axis=1)
    sin = jnp.concatenate([-jnp.sin(row), jnp.sin(row), -jnp.sin(col), jnp.sin(col)], axis=1)
    return cos, sin


def rope_qk(zb, zf, cos, sin, *, tb):
    batch, seq, _ = zb.shape
    return pl.pallas_call(
        _rope_kernel,
        grid=(batch, seq // tb),
        in_specs=[pl.BlockSpec((None, tb, MIX_W), lambda b, t: (b, t, Z_AQ // MIX_W)),
                  pl.BlockSpec((None, tb, KV_W), lambda b, t: (b, t, Z_AK // KV_W)),
                  pl.BlockSpec((tb, HEAD_DIM), lambda b, t: (t, 0)),
                  pl.BlockSpec((tb, HEAD_DIM), lambda b, t: (t, 0))],
        out_specs=[pl.BlockSpec((None, tb, MIX_W), lambda b, t: (b, t, 0)),
                   pl.BlockSpec((None, tb, KV_W), lambda b, t: (b, t, 0))],
        out_shape=[jax.ShapeDtypeStruct((batch, seq, MIX_W), BF16),
                   jax.ShapeDtypeStruct((batch, seq, KV_W), BF16)],
        compiler_params=_cparams("arbitrary", "arbitrary"),
        name="rope_qk",
    )(zb, zf, cos, sin)


def _lat_attn_kernel(q_ref, kp_ref, kc_ref, kn_ref, vp_ref, vc_ref, vn_ref, kx_ref, vx_ref, sink_ref,
                     o_ref, *, seq):
    nq = Q_BLOCK
    past = kx_ref.shape[0]
    scale = HEAD_DIM ** -0.5
    ncol = 3 * nq + past
    jj = lax.broadcasted_iota(jnp.int32, (2 * nq, ncol), 1)
    tt = lax.broadcasted_iota(jnp.int32, (2 * nq, ncol), 0) % nq
    in_band = jnp.abs(tt + nq - jj) <= WINDOW
    units = [(j, kv) for j in range(ATT_QBLOCKS) for kv in range(A_KV)]
    scores, ok = {}, {}
    for j in range(ATT_QBLOCKS):
        kpos = (pl.program_id(1) * ATT_QBLOCKS + j) * nq - nq + jj
        ok[j] = (jj >= 3 * nq) | (in_band & (kpos >= 0) & (kpos < seq))
    for j, kv in units:
        lo = 2 * kv * HEAD_DIM
        ks = slice(kv * HEAD_DIM, (kv + 1) * HEAD_DIM)
        rows = slice(j * nq, (j + 1) * nq)
        kblocks = [kp_ref[:, ks]] + [kc_ref[i * nq:(i + 1) * nq, ks] for i in range(ATT_QBLOCKS)] + [kn_ref[:, ks]]
        q2 = jnp.concatenate([q_ref[rows, lo:lo + HEAD_DIM], q_ref[rows, lo + HEAD_DIM:lo + 2 * HEAD_DIM]], axis=0)
        kcat = jnp.concatenate(kblocks[j:j + 3] + [kx_ref[:, ks].astype(BF16)], axis=0)
        scores[j, kv] = _dot_nt(q2, kcat)
    probs = {}
    for j, kv in units:
        s = jnp.where(ok[j], scores[j, kv] * scale, -jnp.inf)
        probs[j, kv] = _sink_softmax(s, _sink_column(sink_ref, kv, nq)).astype(BF16)
    for j, kv in units:
        lo = 2 * kv * HEAD_DIM
        ks = slice(kv * HEAD_DIM, (kv + 1) * HEAD_DIM)
        rows = slice(j * nq, (j + 1) * nq)
        vblocks = [vp_ref[:, ks]] + [vc_ref[i * nq:(i + 1) * nq, ks] for i in range(ATT_QBLOCKS)] + [vn_ref[:, ks]]
        vcat = jnp.concatenate([blk.astype(BF16) for blk in vblocks[j:j + 3]] + [vx_ref[:, ks].astype(BF16)], axis=0)
        o = _dot(probs[j, kv], vcat)
        o_ref[rows, lo:lo + HEAD_DIM] = o[:nq].astype(BF16)
        o_ref[rows, lo + HEAD_DIM:lo + 2 * HEAD_DIM] = o[nq:].astype(BF16)


def latent_attention(qr, kr, zf, cache_k, cache_v, sink_row, *, layer):
    batch, seq, _ = zf.shape
    nb = seq // Q_BLOCK
    past = cache_k.shape[2]
    wide = ATT_QBLOCKS * Q_BLOCK

    def prev(q):
        return jnp.maximum(q * ATT_QBLOCKS - 1, 0)

    def nxt(q):
        return jnp.minimum((q + 1) * ATT_QBLOCKS, nb - 1)

    vcol = Z_AV // KV_W
    return pl.pallas_call(
        functools.partial(_lat_attn_kernel, seq=seq),
        grid=(batch, nb // ATT_QBLOCKS),
        in_specs=[pl.BlockSpec((None, wide, MIX_W), lambda b, q: (b, q, 0)),
                  pl.BlockSpec((None, Q_BLOCK, KV_W), lambda b, q: (b, prev(q), 0)),
                  pl.BlockSpec((None, wide, KV_W), lambda b, q: (b, q, 0)),
                  pl.BlockSpec((None, Q_BLOCK, KV_W), lambda b, q: (b, nxt(q), 0)),
                  pl.BlockSpec((None, Q_BLOCK, KV_W), lambda b, q: (b, prev(q), vcol)),
                  pl.BlockSpec((None, wide, KV_W), lambda b, q: (b, q, vcol)),
                  pl.BlockSpec((None, Q_BLOCK, KV_W), lambda b, q: (b, nxt(q), vcol)),
                  pl.BlockSpec((None, None, past, KV_W), lambda b, q: (b, layer, 0, 0)),
                  pl.BlockSpec((None, None, past, KV_W), lambda b, q: (b, layer, 0, 0)),
                  pl.BlockSpec((1, 128), lambda b, q: (0, 0))],
        out_specs=pl.BlockSpec((None, wide, MIX_W), lambda b, q: (b, q, 0)),
        out_shape=jax.ShapeDtypeStruct((batch, seq, MIX_W), BF16),
        compiler_params=_cparams("arbitrary", "arbitrary"),
        name="latent_attention",
    )(qr, kr, kr, kr, zf, zf, zf, cache_k, cache_v, sink_row)


def _gated_head_norm(total, gate, nw, act):
    y = total * lax.rsqrt(jnp.mean(total * total, axis=-1, keepdims=True) + EPS)
    return (act(gate) * (y * nw)).astype(BF16)


def _tri_mask(n, rev):
    t = lax.broadcasted_iota(jnp.int32, (n, n), 0)
    s = lax.broadcasted_iota(jnp.int32, (n, n), 1)
    return (s >= t) if rev else (s <= t)


def _chunk_index(rev, n):
    return (lambda c: n - 1 - c) if rev else (lambda c: c)


def _scan_specs(L, width, cidx):
    def zspec(col):
        return pl.BlockSpec((SCAN_SEQS, L, width), lambda b, c: (b, cidx(c), col // width))
    return zspec


def _state_spec(*dims):
    return pl.BlockSpec((SCAN_SEQS,) + dims, lambda b, c: (b,) + (0,) * len(dims))


def _mlstm_kernel(*refs, rev, d, combine):
    if combine:
        (q_ref, k_ref, v_ref, t_ref, bias_ref, c0_ref, n0_ref, m0_ref, hp_ref, g_ref, nw_ref,
         o_ref, c_ref, n_ref, m_ref, cn_ref) = refs
    else:
        (q_ref, k_ref, v_ref, t_ref, bias_ref, c0_ref, n0_ref, m0_ref,
         o_ref, c_ref, n_ref, m_ref, cn_ref) = refs
    L = SCAN_CHUNK
    DK = HEAD_DIM

    @pl.when(pl.program_id(1) == 0)
    def _():
        m_ref[...] = m0_ref[...]
        for sb in range(SCAN_SEQS):
            for h in range(N_HEADS):
                cn_ref[sb, h, :, :DK] = c0_ref[sb, h]
                cn_ref[sb, h, :, DK:] = jnp.broadcast_to(n0_ref[sb, h:h + 1, :], (DK, DK)).T

    tri = _tri_mask(L, rev)
    tri_b = tri.astype(BF16)
    last = 0 if rev else L - 1
    scale = HEAD_DIM ** -0.5
    gates, cum = {}, {}
    for sb in range(SCAN_SEQS):
        gates[sb] = t_ref[sb] + bias_ref[...]
        hi, mid, lo = _split3(_log_sigmoid(gates[sb]))
        cum[sb] = _dot(tri_b, hi) + _dot(tri_b, mid) + _dot(tri_b, lo)
    units = [(sb, h) for sb in range(SCAN_SEQS) for h in range(N_HEADS)]
    gates_t = {sb: gates[sb].T for sb in range(SCAN_SEQS)}
    cum_t = {sb: cum[sb].T for sb in range(SCAN_SEQS)}
    m_all = {sb: m_ref[sb] for sb in range(SCAN_SEQS)}
    ones = jnp.ones((L, DK), BF16)
    col, qb, ks, v1 = {}, {}, {}, {}
    for u in units:
        sb, h = u
        ci, cf = d * 8 + h, d * 8 + 4 + h
        sl = slice(h * HEAD_DIM, (h + 1) * HEAD_DIM)
        i_col, b_col = gates[sb][:, ci:ci + 1], cum[sb][:, cf:cf + 1]
        m = m_all[sb][:, h:h + 1]
        bl = b_col[last:last + 1, :]
        g = bl - b_col + i_col
        m_new = jnp.maximum(bl + m, jnp.max(g, axis=0, keepdims=True))
        col[u] = dict(m=m, b=b_col, m_new=m_new, w_old=jnp.exp(bl + m - m_new), e=jnp.exp(g - m_new))
        qb[u] = q_ref[sb, :, sl].astype(BF16)
        ks[u] = k_ref[sb, :, sl].astype(F32) * scale
        v1[u] = jnp.concatenate([v_ref[sb, :, sl].astype(BF16), ones], axis=1)
    kw_t = {u: (col[u]["e"] * ks[u]).T.astype(BF16) for u in units}
    s_raw, q_cn = {}, {}
    for u in units:
        sb, h = u
        cn_old = cn_ref[sb, h]
        s_raw[u] = _dot_nt(qb[u], ks[u].astype(BF16))
        q_cn[u] = _dot(qb[u], cn_old.astype(BF16))
        cn_ref[sb, h] = col[u]["w_old"] * cn_old + _dot(kw_t[u], v1[u])
    wmat = {}
    for u in units:
        sb, h = u
        ci, cf = d * 8 + h, d * 8 + 4 + h
        r_row = gates_t[sb][ci:ci + 1, :] - cum_t[sb][cf:cf + 1, :]
        wmat[u] = jnp.where(tri, r_row, -jnp.inf)
    cmax = {u: jnp.max(wmat[u], axis=1, keepdims=True) for u in units}
    u_bc, s = {}, {}
    for u in units:
        u_bc[u] = jnp.broadcast_to(jnp.maximum(col[u]["m"], cmax[u]), (L, L))
        s[u] = (s_raw[u] * jnp.exp(wmat[u] - u_bc[u])).astype(BF16)
    sv1 = {u: _dot(s[u], v1[u]) for u in units}
    for u in units:
        sb, h = u
        sl = slice(h * HEAD_DIM, (h + 1) * HEAD_DIM)
        w_state = jnp.exp(col[u]["m"] - u_bc[u])
        den = w_state * q_cn[u][:, DK:] + sv1[u][:, DK:]
        floor = jnp.exp(-(jnp.broadcast_to(col[u]["b"], (L, L)) + u_bc[u]))
        hh = (w_state * q_cn[u][:, :DK] + sv1[u][:, :DK]) * (1.0 / jnp.maximum(jnp.abs(den), floor))
        if combine:
            o_ref[sb, :, sl] = _gated_head_norm(hh + hp_ref[sb, :, sl], g_ref[sb, :, sl], nw_ref[:, sl],
                                                jax.nn.sigmoid)
        else:
            o_ref[sb, :, sl] = hh
    for u in units:
        sb, h = u
        m_ref[sb, :, h:h + 1] = col[u]["m_new"]

    @pl.when(pl.program_id(1) == pl.num_programs(1) - 1)
    def _():
        for sb in range(SCAN_SEQS):
            for h in range(N_HEADS):
                c_ref[sb, h] = cn_ref[sb, h, :, :DK]
                n_ref[sb, h:h + 1, :] = cn_ref[sb, h, :, DK:].T[0:1, :]


def mlstm_direction(zb, zf, bias_row, c0, n0, m0, *, d, prev=None, norm_w=None):
    batch, seq, _ = zb.shape
    L =SCAN_CHUNK
    nc = seq // L
    rev = d == 1
    cidx = _chunk_index(rev, nc)
    combine = prev is not None
    zspec = _scan_specs(L, MIX_W, cidx)
    state_specs = [_state_spec(N_HEADS, HEAD_DIM, HEAD_DIM), _state_spec(N_HEADS, HEAD_DIM),
                   _state_spec(1, N_HEADS)]
    in_specs = [zspec(Z_MQ), zspec(Z_MK), zspec(Z_MV),
                pl.BlockSpec((SCAN_SEQS, L, 128), lambda b, c: (b, cidx(c), Z_TAIL // 128)),
                pl.BlockSpec((1, 128), lambda b, c: (0, 0))] + state_specs
    args = [zb, zb, zb, zf, bias_row, c0, n0, m0]
    if combine:
        in_specs += [zspec(0), zspec(Z_MO), pl.BlockSpec((1, MIX_W), lambda b, c: (0, 0))]
        args += [prev, zf, norm_w]
    return pl.pallas_call(
        functools.partial(_mlstm_kernel, rev=rev, d=d, combine=combine),
        grid=(batch // SCAN_SEQS, nc),
        in_specs=in_specs,
        out_specs=[zspec(0)] + state_specs,
        out_shape=[jax.ShapeDtypeStruct((batch, seq, MIX_W), BF16 if combine else F32),
                   jax.ShapeDtypeStruct((batch, N_HEADS, HEAD_DIM, HEAD_DIM), F32),
                   jax.ShapeDtypeStruct((batch, N_HEADS, HEAD_DIM), F32),
                   jax.ShapeDtypeStruct((batch, 1, N_HEADS), F32)],
        scratch_shapes=[pltpu.VMEM((SCAN_SEQS, N_HEADS, HEAD_DIM, 2 * HEAD_DIM), F32)],
        compiler_params=_cparams("arbitrary", "arbitrary"),
        name="mlstm_scan",
    )(*args)


def _retention_kernel(*refs, rev, combine):
    if combine:
        q_ref, k_ref, v_ref, dec_ref, s0_ref, hp_ref, g_ref, nw_ref, o_ref, s_ref = refs
    else:
        q_ref, k_ref, v_ref, dec_ref, s0_ref, o_ref, s_ref = refs
    L = SCAN_CHUNK

    @pl.when(pl.program_id(1) == 0)
    def _():
        s_ref[...] = s0_ref[...]

    tri = _tri_mask(L, rev)
    t_i = lax.broadcasted_iota(jnp.int32, (L, L), 0)
    s_i = lax.broadcasted_iota(jnp.int32, (L, L), 1)
    dist = jnp.abs(t_i - s_i).astype(F32)
    pos = lax.broadcasted_iota(jnp.int32, (L, 1), 0).astype(F32)
    xi_pow = (L - pos) if rev else (pos + 1.0)
    zeta_pow = pos if rev else (L - 1.0 - pos)
    log_gamma = _log_sigmoid(dec_ref[...])
    scale = HEAD_DIM ** -0.5
    units = [(sb, h) for h in range(N_HEADS) for sb in range(SCAN_SEQS)]
    stage1 = {}
    for sb, h in units:
        sl = slice(h * HEAD_DIM, (h + 1) * HEAD_DIM)
        lg = log_gamma[:, h:h + 1]
        qb = q_ref[sb, :, sl].astype(BF16)
        ks = k_ref[sb, :, sl].astype(F32) * scale
        vb = v_ref[sb, :, sl].astype(BF16)
        s_old = s_ref[sb, h]
        att = _dot_nt(qb, ks.astype(BF16))
        inter = _dot(qb, s_old.astype(BF16))
        outer = _dot((ks * jnp.exp(zeta_pow * lg)).T.astype(BF16), vb)
        s_ref[sb, h] = jnp.exp(L * lg) * s_old + outer
        stage1[sb, h] = (att, inter, vb)
    decay, xi = {}, {}
    for h in range(N_HEADS):
        lg = log_gamma[:, h:h + 1]
        decay[h] = jnp.where(tri, jnp.exp(dist * lg), 0.0)
        xi[h] = jnp.exp(xi_pow * lg)
    for sb, h in units:
        sl = slice(h * HEAD_DIM, (h + 1) * HEAD_DIM)
        att, inter, vb = stage1[sb, h]
        y = _dot((att * decay[h]).astype(BF16), vb) + xi[h] * inter
        if combine:
            o_ref[sb, :, sl] = _gated_head_norm(y + hp_ref[sb, :, sl], g_ref[sb, :, sl], nw_ref[:, sl],
                                                jax.nn.silu)
        else:
            o_ref[sb, :, sl] = y


def retention_direction(zb, zf, decay_row, s0, *, d, prev=None, norm_w=None):
    batch, seq, _ = zb.shape
    L =SCAN_CHUNK
    nc = seq // L
    rev = d == 1
    cidx = _chunk_index(rev, nc)
    combine = prev is not None
    zspec = _scan_specs(L, MIX_W, cidx)
    state_spec = _state_spec(N_HEADS, HEAD_DIM, HEAD_DIM)
    in_specs = [zspec(Z_RQ), zspec(Z_RK), zspec(Z_RV), pl.BlockSpec((1, 128), lambda b, c: (0, 0)), state_spec]
    args = [zb, zb, zb, decay_row, s0]
    if combine:
        in_specs += [zspec(0), zspec(Z_RG), pl.BlockSpec((1, MIX_W), lambda b, c: (0, 0))]
        args += [prev, zf, norm_w]
    return pl.pallas_call(
        functools.partial(_retention_kernel, rev=rev, combine=combine),
        grid=(batch // SCAN_SEQS, nc),
        in_specs=in_specs,
        out_specs=[zspec(0), state_spec],
        out_shape=[jax.ShapeDtypeStruct((batch, seq, MIX_W), BF16 if combine else F32),
                   jax.ShapeDtypeStruct((batch, N_HEADS, HEAD_DIM, HEAD_DIM), F32)],
        compiler_params=_cparams("arbitrary", "arbitrary"),
        name="retention_scan",
    )(*args)


GLA_LEVELS = (32, 16, 8, 4, 2, 1)
GLA_LANES = N_HEADS * G_DK


def _gla_constants(rev):
    L = GLA_CHUNK
    t = np.arange(L)[:, None]
    r = np.arange(L)[None, :]
    mats = [(r <= t).astype(np.float32)]
    pair = [np.eye(L, dtype=np.float32)]
    for w in GLA_LEVELS:
        blk_t, blk_r = t // (2 * w), r // (2 * w)
        ref_t = blk_t * 2 * w + w - 1
        upper_t = (t % (2 * w)) >= w
        mats.append((upper_t & (r > ref_t) & (r <= t)).astype(np.float32))
        mats.append(((~upper_t) & (r > t) & (r <= ref_t)).astype(np.float32))
        lower_r = (r % (2 * w)) < w
        pair.append((upper_t & lower_r & (blk_t == blk_r)).astype(np.float32))
    if rev:
        mats = [m[::-1, ::-1] for m in mats]
        pair = [p[::-1, ::-1] for p in pair]
    big = np.concatenate(mats, axis=0)
    big3 = np.concatenate([big, big, big], axis=1)
    pair = np.stack([np.tile(p, (1, 2)) for p in pair])
    return jnp.asarray(big3, BF16), jnp.asarray(pair, F32)


def _gla_kernel(*refs, rev, combine):
    if combine:
        (q_ref, k_ref, v_ref, t_ref, w2_ref, gb_ref, big_ref, pair_ref, s0_ref,
         hp_ref, g_ref, nw_ref, o_ref, so_ref, s_ref) = refs
    else:
        (q_ref, k_ref, v_ref, t_ref, w2_ref, gb_ref, big_ref, pair_ref, s0_ref,
         o_ref, so_ref, s_ref) = refs
    L = GLA_CHUNK
    c = pl.program_id(1)
    pairs = N_HEADS // 2
    pw = 2 * G_DK
    ow = 2 * HEAD_DIM

    @pl.when(c == 0)
    def _():
        s_ref[...] = jnp.zeros_like(s_ref)
        for sb in range(SCAN_SEQS):
            for h in range(N_HEADS):
                hl = h % 2
                s_ref[sb, h // 2, hl * G_DK:(hl + 1) * G_DK, hl * HEAD_DIM:(hl + 1) * HEAD_DIM] = s0_ref[sb, h]

    last = 0 if rev else L - 1
    first_head = lax.broadcasted_iota(jnp.int32, (L, pw), 1) < G_DK
    diag_block = ((lax.broadcasted_iota(jnp.int32, (pw, ow), 0) < G_DK)
                  == (lax.broadcasted_iota(jnp.int32, (pw, ow), 1) < HEAD_DIM))
    zeros_v = jnp.zeros((L, HEAD_DIM), BF16)

    def stacked(x):
        return jnp.concatenate([jnp.where(first_head, x, 0.0), jnp.where(first_head, 0.0, x)], axis=0).astype(BF16)

    seqs = range(SCAN_SEQS)
    units = [(sb, p) for sb in seqs for p in range(pairs)]
    logit = {sb: _dot(t_ref[sb].astype(BF16), w2_ref[...]) + gb_ref[...] for sb in seqs}
    e = {}
    for sb in seqs:
        la = _log_sigmoid(logit[sb]) * (1.0 / GLA_TAU)
        e[sb] = _dot(big_ref[...], jnp.concatenate(_split3(la), axis=0))
    pad = jnp.zeros((128 - L, pw), F32)
    pad_v = jnp.zeros((128 - L, ow), BF16)
    stage = {}
    for sb, p in units:
        lanes = slice(p * pw, (p + 1) * pw)
        cum = e[sb][0:L, lanes]
        q = q_ref[sb, :, lanes].astype(F32) * (G_DK ** -0.5)
        k = k_ref[sb, :, lanes].astype(F32)
        vb = v_ref[sb, :, p * ow:(p + 1) * ow].astype(BF16)
        s_old = s_ref[sb, p]
        att = pair_ref[0] * _dot_nt(q.astype(BF16), stacked(k))
        for i in range(len(GLA_LEVELS)):
            eq = e[sb][(2 * i + 1) * L:(2 * i + 2) * L, lanes]
            ek = e[sb][(2 * i + 2) * L:(2 * i + 3) * L, lanes]
            att = att + pair_ref[i + 1] * _dot_nt((q * jnp.exp(eq)).astype(BF16), stacked(k * jnp.exp(ek)))
        inter = _dot((q * jnp.exp(cum)).astype(BF16), s_old.astype(BF16))
        last_row = cum[last:last + 1, :]
        kd_t = jnp.concatenate([k * jnp.exp(last_row - cum), pad], axis=0).T
        cum_t = jnp.concatenate([cum, pad], axis=0).T
        upd = _dot(kd_t.astype(BF16), jnp.concatenate([vb, pad_v], axis=0))
        s_ref[sb, p] = jnp.exp(cum_t[:, last:last + 1]) * s_old + jnp.where(diag_block, upd, 0.0)
        stage[sb, p] = (att.astype(BF16), inter, vb)
    for sb, p in units:
        att, inter, vb = stage[sb, p]
        vblk = jnp.concatenate([jnp.concatenate([vb[:, :HEAD_DIM], zeros_v], axis=1),
                                jnp.concatenate([zeros_v, vb[:, HEAD_DIM:]], axis=1)], axis=0)
        y = _dot(att, vblk) + inter
        for hl in range(2):
            sl = slice(p * ow + hl * HEAD_DIM, p * ow + (hl + 1) * HEAD_DIM)
            yh = y[:, hl * HEAD_DIM:(hl + 1) * HEAD_DIM]
            if combine:
                o_ref[sb, :, sl] = _gated_head_norm(yh + hp_ref[sb, :, sl], g_ref[sb, :, sl], nw_ref[:, sl],
                                                    jax.nn.silu)
            else:
                o_ref[sb, :, sl] = yh

    @pl.when(c == pl.num_programs(1) - 1)
    def _():
        for sb in range(SCAN_SEQS):
            for h in range(N_HEADS):
                hl = h % 2
                so_ref[sb, h] = s_ref[sb, h // 2, hl * G_DK:(hl + 1) * G_DK, hl * HEAD_DIM:(hl + 1) * HEAD_DIM]


def gla_direction(zb, zf, w2pad, gbias, s0, *, d, prev=None, norm_w=None):
    batch, seq, _ = zb.shape
    L =GLA_CHUNK
    nc = seq // L
    rev = d == 1
    cidx = _chunk_index(rev, nc)
    combine = prev is not None
    big3, pair = _gla_constants(rev)

    def const(a):
        return pl.BlockSpec(a.shape, lambda b, c: (0,) * a.ndim)

    zq = _scan_specs(L, GLA_LANES, cidx)
    zv = _scan_specs(L, MIX_W, cidx)
    state_spec = _state_spec(N_HEADS, G_DK, HEAD_DIM)
    in_specs = [zq(Z_GQ), zq(Z_GK), zv(Z_GV),
                pl.BlockSpec((SCAN_SEQS, L, 128), lambda b, c: (b, cidx(c), Z_TAIL // 128)),
                const(w2pad), const(gbias), const(big3), const(pair), state_spec]
    args = [zb, zb, zb, zf, w2pad, gbias, big3, pair, s0]
    if combine:
        in_specs += [zv(0), zv(Z_GG), pl.BlockSpec((1, MIX_W), lambda b, c: (0, 0))]
        args += [prev, zf, norm_w]
    return pl.pallas_call(
        functools.partial(_gla_kernel, rev=rev, combine=combine),
        grid=(batch // SCAN_SEQS, nc),
        in_specs=in_specs,
        out_specs=[zv(0), state_spec],
        out_shape=[jax.ShapeDtypeStruct((batch, seq, MIX_W), BF16 if combine else F32),
                   jax.ShapeDtypeStruct((batch, N_HEADS, G_DK, HEAD_DIM), F32)],
        scratch_shapes=[pltpu.VMEM((SCAN_SEQS, N_HEADS // 2, 2 * G_DK, 2 * HEAD_DIM), F32)],
        compiler_params=_cparams("arbitrary", "arbitrary"),
        name="gla_scan",
    )(*args)


def _pad_row(v, width=128):
    v = v.reshape(1, -1).astype(F32)
    return jnp.pad(v, ((0, 0), (0, width - v.shape[1])))


def _permute_w_in(w_in_l):
    sizes = (512, 256, 256, 512, 512, 512, 512, 16, 512, 512, 512, 512, 256, 256, 512, 512, 32)
    offs = np.concatenate([[0], np.cumsum(sizes)])
    pieces = [w_in_l[:, offs[i]:offs[i + 1]] for i in range(len(sizes))]
    order = [0, 3, 4, 5, 8, 9, 10, 12, 13, 14,
             1, 2, 6, 11, 15, 7, 16]
    w = jnp.concatenate([pieces[i] for i in order], axis=1)
    return jnp.pad(w, ((0, 0), (0, ZB_COLS + ZF_COLS - w.shape[1]))).astype(BF16)


def _prepare_layer(p, l):
    d = p["w_in"].shape[1]
    w2pad = []
    for direction in range(2):
        lo = TAIL_GLR + direction * G_RANK
        w2pad.append(jnp.zeros((128, GLA_LANES), F32).at[lo:lo + G_RANK].set(p["gla_w2"][l, direction]).astype(BF16))
    return {
        "w_in": _permute_w_in(p["w_in"][l]),
        "w_mgate": p["w_mgate"][l].astype(BF16), "w_br": p["w_br"][l].astype(BF16),
        "w_out": p["w_out"][l].astype(BF16), "ffn_w_gu": p["ffn_w_gu"][l].astype(BF16),
        "ffn_w_down": p["ffn_w_down"][l].astype(BF16),
        "norm1_w": p["norm1_w"][l].reshape(1, d), "norm2_w": p["norm2_w"][l].reshape(1, d),
        "sink_row": _pad_row(p["attn_sink"][l]), "mlstm_bias": _pad_row(p["mlstm_if_b"][l]),
        "mlstm_norm_w": p["mlstm_norm_w"][l].reshape(1, MIX_W),
        "ret_decay": [_pad_row(p["ret_decay"][l, 0]), _pad_row(p["ret_decay"][l, 1])],
        "ret_norm_w": p["ret_norm_w"][l].reshape(1, MIX_W),
        "gla_w2": w2pad, "gla_b": [p["gla_b"][l, 0].reshape(1, GLA_LANES), p["gla_b"][l, 1].reshape(1, GLA_LANES)],
        "gla_norm_w": p["gla_norm_w"][l].reshape(1, MIX_W),
    }


def _mixers(zb, zf, w, l, cache):
    batch, seq, _ = zb.shape
    is_ctx = cache is None
    if is_ctx:
        ya = context_attention(zb, zf, w["sink_row"])
    else:
        cos, sin = rope_tables(seq)
        qr, kr = rope_qk(zb, zf, cos, sin, tb=min(seq, 512))
        ck = cache["attn_k"].reshape(cache["attn_k"].shape[:3] + (KV_W,))
        cv = cache["attn_v"].reshape(cache["attn_v"].shape[:3] + (KV_W,))
        ya = latent_attention(qr, kr, zf, ck, cv, w["sink_row"], layer=l)

    def zeros(*dims):
        return jnp.zeros((batch,) + dims, F32)

    res = {}
    for d in (1, 0):
        if is_ctx:
            init = (zeros(N_HEADS, HEAD_DIM, HEAD_DIM), zeros(N_HEADS, HEAD_DIM), zeros(1, N_HEADS))
        else:
            init = (cache["mlstm_C"][:, l, d], cache["mlstm_n"][:, l, d],
                    cache["mlstm_m"][:, l, d].reshape(batch, 1, N_HEADS))
        res[d] = mlstm_direction(zb, zf, w["mlstm_bias"], *init, d=d, prev=res[1][0] if d == 0 else None,
                                 norm_w=w["mlstm_norm_w"] if d == 0 else None)
    ym = res[0][0]
    states = {"mlstm_C": jnp.stack([res[0][1], res[1][1]], axis=1),
              "mlstm_n": jnp.stack([res[0][2], res[1][2]], axis=1),
              "mlstm_m": jnp.stack([res[0][3][:, 0], res[1][3][:, 0]], axis=1)}

    res = {}
    for d in (1, 0):
        s0 = zeros(N_HEADS, HEAD_DIM, HEAD_DIM) if is_ctx else cache["ret_S"][:, l, d]
        res[d] = retention_direction(zb, zf, w["ret_decay"][d], s0, d=d, prev=res[1][0] if d == 0 else None,
                                     norm_w=w["ret_norm_w"] if d == 0 else None)
    yr = res[0][0]
    states["ret_S"] = jnp.stack([res[0][1], res[1][1]], axis=1)

    res = {}
    for d in (1, 0):
        s0 = zeros(N_HEADS, G_DK, HEAD_DIM) if is_ctx else cache["gla_S"][:, l, d]
        res[d] = gla_direction(zb, zf, w["gla_w2"][d], w["gla_b"][d], s0, d=d, prev=res[1][0] if d == 0 else None,
                               norm_w=w["gla_norm_w"] if d == 0 else None)
    yg = res[0][0]
    states["gla_S"] = jnp.stack([res[0][1], res[1][1]], axis=1)

    rows = batch * seq
    ys = [y.reshape(rows, MIX_W) for y in (ya, ym, yr, yg)]
    states["attn_k"] = zf[:, :, Z_AK:Z_AK + KV_W].reshape(batch, seq, A_KV, HEAD_DIM)
    states["attn_v"] = zf[:, :, Z_AV:Z_AV + KV_W].reshape(batch, seq, A_KV, HEAD_DIM)
    return ys, states


def _layer(x, dims, mod_l, w, l, cache, *, grp, tm):
    batch, seq = dims
    d = x.shape[1]
    sh1, sc1, g1, sh2, sc2, g2 = (mod_l[:, i * d:(i + 1) * d] for i in range(6))
    zb, zf, h = in_projection(x, sh1, sc1, w["norm1_w"], w["w_in"], grp=grp, tm=tm)
    ys, states = _mixers(zb.reshape(batch, seq, ZB_COLS), zf.reshape(batch, seq, ZF_COLS), w, l, cache)
    merged = merge_branches(h, ys, w["w_mgate"], w["w_br"], tm=tm, tn=256)
    x, h2 = out_projection_norm(merged, w["w_out"], x, g1, sh2, sc2, w["norm2_w"], grp=grp, tm=tm // 2)
    act = ffn_up(h2, w["ffn_w_gu"], tm=tm, tn=512)
    x = residual_projection(act, w["ffn_w_down"], x, g2, grp=grp, tm=tm, tn=256)
    return x, states


def _forward(x_prompt, x_sample, cache, c, c_ctx, p, final_norm_w, *, tm):
    cb, ct, d = x_prompt.shape
    lb, lt, _ = x_sample.shape
    assert (cb * ct) % tm == 0 and lt % tm == 0 and lb + 1 <= MOD_ROWS
    assert cb % SCAN_SEQS == 0 and lb % SCAN_SEQS == 0
    depth = p["w_ada"].shape[0]
    cond = jnp.concatenate([c_ctx[None, :], c, jnp.zeros((MOD_ROWS - 1 - lb, d), F32)], axis=0)
    mod = ada_modulation(cond, p["w_ada"], p["b_ada"])
    xc = x_prompt.reshape(cb * ct, d)
    xl = x_sample.reshape(lb * lt, d)
    ctx_states = []
    for l in range(depth):
        w = _prepare_layer(p, l)
        xc, st = _layer(xc, (cb, ct), mod[l], w, l, None, grp=(0, cb * ct), tm=tm)
        xl, _ = _layer(xl, (lb, lt), mod[l], w, l, cache, grp=(1, lt), tm=tm)
        ctx_states.append(st)
    fw = final_norm_w.reshape(1, d)
    y_prompt = final_norm(xc, fw, tm=tm).reshape(cb, ct, d)
    y_sample = final_norm(xl, fw, tm=tm).reshape(lb, lt, d)

    def stack(name):
        return jnp.stack([s[name] for s in ctx_states], axis=1)

    return (y_prompt, y_sample, stack("attn_k"), stack("attn_v"), stack("mlstm_C"), stack("mlstm_n"),
            stack("mlstm_m"), stack("ret_S"), stack("gla_S"))


def kernel(x_prompt, x_sample, cache_attn_k, cache_attn_v, state_mlstm_C, state_mlstm_n, state_mlstm_m, state_ret_S, state_gla_S, c, c_ctx, w_ada, b_ada, norm1_w, norm2_w, w_in, attn_sink, mlstm_if_b, mlstm_norm_w, ret_decay, ret_norm_w, gla_w2, gla_b, gla_norm_w, w_br, w_mgate, w_out, ffn_w_gu, ffn_w_down, final_norm_w):
    p = {"w_ada": w_ada, "b_ada": b_ada, "norm1_w": norm1_w, "norm2_w": norm2_w, "w_in": w_in,
         "attn_sink": attn_sink, "mlstm_if_b": mlstm_if_b, "mlstm_norm_w": mlstm_norm_w,
         "ret_decay": ret_decay, "ret_norm_w": ret_norm_w, "gla_w2": gla_w2, "gla_b": gla_b,
         "gla_norm_w": gla_norm_w, "w_br": w_br, "w_mgate": w_mgate, "w_out": w_out,
         "ffn_w_gu": ffn_w_gu, "ffn_w_down": ffn_w_down}
    cache = {"attn_k": cache_attn_k, "attn_v": cache_attn_v, "mlstm_C": state_mlstm_C,
             "mlstm_n": state_mlstm_n, "mlstm_m": state_mlstm_m, "ret_S": state_ret_S, "gla_S": state_gla_S}
    return _forward(x_prompt, x_sample, cache, c, c_ctx, p, final_norm_w, tm=1024)
```

```python
import functools

import numpy as np
import jax
import jax.numpy as jnp
from jax import lax
from jax.experimental import pallas as pl
from jax.experimental.pallas import tpu as pltpu

F32 = jnp.float32
BF16 = jnp.bfloat16

HEAD_DIM = 128
N_HEADS = 4
A_KV = 2
GRID_W = 64
WINDOW = 128
Q_BLOCK = 128
G_DK = 64
G_RANK = 16
GLA_TAU = 16.0
ROPE_BASE = 10000.0
EPS = 1e-6
MOD_ROWS = 16
SCAN_CHUNK = 128
GLA_CHUNK = 64
SCAN_SEQS = 4
ATT_QBLOCKS = 2
NORM_ROWS = 64
MIX_W = N_HEADS * HEAD_DIM
KV_W = A_KV * HEAD_DIM

Z_AQ = 0
Z_MQ, Z_MK, Z_MV = 512, 1024, 1536
Z_RQ, Z_RK, Z_RV = 2048, 2560, 3072
Z_GQ, Z_GK, Z_GV = 3584, 3840, 4096
ZB_COLS = 4608
Z_AK, Z_AV = 0, 256
Z_MO, Z_RG, Z_GG = 512, 1024, 1536
Z_TAIL = 2048
ZF_COLS = 2304
Z_TN = 1152
TAIL_GLR = 16

VMEM_LIMIT = 56 * 2 ** 20


def _cparams(*sem):
    return pltpu.CompilerParams(dimension_semantics=sem, vmem_limit_bytes=VMEM_LIMIT)


def _log_sigmoid(x):
    return jnp.minimum(x, 0.0) - jnp.log1p(jnp.exp(-jnp.abs(x)))


def _split3(x):
    hi = x.astype(BF16)
    r = x - hi.astype(F32)
    mid = r.astype(BF16)
    lo = (r - mid.astype(F32)).astype(BF16)
    return hi, mid, lo


def _dot(a, b):
    return jnp.dot(a, b, preferred_element_type=F32)


def _dot_nt(a, b):
    return lax.dot_general(a, b, (((1,), (1,)), ((), ())), preferred_element_type=F32)


def _ada_kernel(cond_ref, w_ref, b_ref, o_ref):
    s = jax.nn.silu(cond_ref[...]).astype(BF16)
    o_ref[0] = _dot(s, w_ref[0].astype(BF16)) + b_ref[0]


def ada_modulation(cond, w_ada, b_ada):
    depth, d, n = w_ada.shape
    tn = 1024
    return pl.pallas_call(
        _ada_kernel,
        grid=(depth, n // tn),
        in_specs=[pl.BlockSpec((MOD_ROWS, d), lambda l, j: (0, 0)),
                  pl.BlockSpec((1, d, tn), lambda l, j: (l, 0, j)),
                  pl.BlockSpec((1, 1, tn), lambda l, j: (l, 0, j))],
        out_specs=pl.BlockSpec((1, MOD_ROWS, tn), lambda l, j: (l, 0, j)),
        out_shape=jax.ShapeDtypeStruct((depth, MOD_ROWS, n), F32),
        compiler_params=_cparams("arbitrary", "arbitrary"),
        name="ada_modulation",
    )(cond, w_ada, b_ada.reshape(depth, 1, n))


def _mod_row(grp, tm):
    return grp[0] + (pl.program_id(0) * tm) // grp[1]


def _modulated_norm(x, nw, sh, sc):
    y = x * lax.rsqrt(jnp.mean(x * x, axis=-1, keepdims=True) + EPS)
    return (y * nw) * (1.0 + sc) + sh


def _in_proj_kernel(x_ref, sh_ref, sc_ref, nw_ref, w_ref, zb_ref, zf_ref, h_ref, *, tm, grp, nb):
    j = pl.program_id(1)

    @pl.when(j == 0)
    def _():
        g = _mod_row(grp, tm)
        nw, sh, sc = nw_ref[...], sh_ref[pl.ds(g, 1), :], sc_ref[pl.ds(g, 1), :]
        for r in range(0, tm, NORM_ROWS):
            rows = slice(r, r + NORM_ROWS)
            h_ref[rows, :] = _modulated_norm(x_ref[rows, :], nw, sh, sc).astype(BF16)

    z = _dot(h_ref[...], w_ref[...])

    @pl.when(j < nb)
    def _():
        zb_ref[...] = z.astype(BF16)

    @pl.when(j >= nb)
    def _():
        zf_ref[...] = z


def in_projection(x, sh, sc, nw, w, *, grp, tm):
    rows, d = x.shape
    nb, nf = ZB_COLS // Z_TN, ZF_COLS // Z_TN
    return pl.pallas_call(
        functools.partial(_in_proj_kernel, tm=tm, grp=grp, nb=nb),
        grid=(rows // tm, nb + nf),
        in_specs=[pl.BlockSpec((tm, d), lambda i, j: (i, 0)),
                  pl.BlockSpec((MOD_ROWS, d), lambda i, j: (0, 0)),
                  pl.BlockSpec((MOD_ROWS, d), lambda i, j: (0, 0)),
                  pl.BlockSpec((1, d), lambda i, j: (0, 0)),
                  pl.BlockSpec((d, Z_TN), lambda i, j: (0, j))],
        out_specs=[pl.BlockSpec((tm, Z_TN), lambda i, j: (i, jnp.minimum(j, nb - 1))),
                   pl.BlockSpec((tm, Z_TN), lambda i, j: (i, jnp.maximum(j - nb, 0))),
                   pl.BlockSpec((tm, d), lambda i, j: (i, 0))],
        out_shape=[jax.ShapeDtypeStruct((rows, ZB_COLS), BF16), jax.ShapeDtypeStruct((rows, ZF_COLS), F32),
                   jax.ShapeDtypeStruct((rows, d), BF16)],
        compiler_params=_cparams("arbitrary", "arbitrary"),
        name="in_projection",
    )(x, sh, sc, nw, w)


def _merge_kernel(h_ref, ya_ref, ym_ref, yr_ref, yg_ref, wg_ref, wb_ref, o_ref):
    h = h_ref[...]
    acc = None
    for b, y_ref in enumerate((ya_ref, ym_ref, yr_ref, yg_ref)):
        gate = jax.nn.sigmoid(_dot(h, wg_ref[b]))
        term = gate * _dot(y_ref[...], wb_ref[b])
        acc = term if acc is None else acc + term
    o_ref[...] = acc.astype(BF16)


def merge_branches(h, ys, w_mgate, w_br, *, tm, tn):
    rows, d = h.shape
    nb, mw, _ = w_br.shape
    return pl.pallas_call(
        _merge_kernel,
        grid=(rows // tm, d // tn),
        in_specs=[pl.BlockSpec((tm, d), lambda i, j: (i, 0))]
        + [pl.BlockSpec((tm, mw), lambda i, j: (i, 0))] * nb
        + [pl.BlockSpec((nb, d, tn), lambda i, j: (0, 0, j)),
           pl.BlockSpec((nb, mw, tn), lambda i, j: (0, 0, j))],
        out_specs=pl.BlockSpec((tm, tn), lambda i, j: (i, j)),
        out_shape=jax.ShapeDtypeStruct((rows, d), BF16),
        compiler_params=_cparams("arbitrary", "arbitrary"),
        name="merge_branches",
    )(h, *ys, w_mgate, w_br)


def _residual_proj_kernel(a_ref, w_ref, x_ref, g_ref, o_ref, *, tm, grp):
    g = _mod_row(grp, tm)
    o_ref[...] = x_ref[...] + g_ref[pl.ds(g, 1), :] * _dot(a_ref[...], w_ref[...])


def residual_projection(a, w, x, gate, *, grp, tm, tn):
    rows, k = a.shape
    n = w.shape[1]
    return pl.pallas_call(
        functools.partial(_residual_proj_kernel, tm=tm, grp=grp),
        grid=(rows // tm, n // tn),
        in_specs=[pl.BlockSpec((tm, k), lambda i, j: (i, 0)),
                  pl.BlockSpec((k, tn), lambda i, j: (0, j)),
                  pl.BlockSpec((tm, tn), lambda i, j: (i, j)),
                  pl.BlockSpec((MOD_ROWS, tn), lambda i, j: (0, j))],
        out_specs=pl.BlockSpec((tm, tn), lambda i, j: (i, j)),
        out_shape=jax.ShapeDtypeStruct((rows, n), F32),
        compiler_params=_cparams("arbitrary", "arbitrary"),
        name="residual_projection",
    )(a, w, x, gate)


def _out_proj_norm_kernel(a_ref, w_ref, x_ref, g_ref, sh_ref, sc_ref, nw_ref, o_ref, h_ref, *, tm, grp):
    g = _mod_row(grp, tm)
    x = x_ref[...] + g_ref[pl.ds(g, 1), :] * _dot(a_ref[...], w_ref[...])
    o_ref[...] = x
    h_ref[...] = _modulated_norm(x, nw_ref[...], sh_ref[pl.ds(g, 1), :], sc_ref[pl.ds(g, 1), :]).astype(BF16)


def out_projection_norm(a, w, x, gate, sh, sc, nw, *, grp, tm):
    rows, k = a.shape
    d = w.shape[1]
    full = lambda i: (0, 0)
    return pl.pallas_call(
        functools.partial(_out_proj_norm_kernel, tm=tm, grp=grp),
        grid=(rows // tm,),
        in_specs=[pl.BlockSpec((tm, k), lambda i: (i, 0)),
                  pl.BlockSpec((k, d), full),
                  pl.BlockSpec((tm, d), lambda i: (i, 0)),
                  pl.BlockSpec((MOD_ROWS, d), full),
                  pl.BlockSpec((MOD_ROWS, d), full),
                  pl.BlockSpec((MOD_ROWS, d), full),
                  pl.BlockSpec((1, d), full)],
        out_specs=[pl.BlockSpec((tm, d), lambda i: (i, 0)), pl.BlockSpec((tm, d), lambda i: (i, 0))],
        out_shape=[jax.ShapeDtypeStruct((rows, d), F32), jax.ShapeDtypeStruct((rows, d), BF16)],
        compiler_params=_cparams("arbitrary"),
        name="out_projection_norm",
    )(a, w, x, gate, sh, sc, nw)


def _ffn_up_kernel(h_ref, wg_ref, wu_ref, o_ref):
    h = h_ref[...]
    o_ref[...] = (jax.nn.silu(_dot(h, wg_ref[...])) * _dot(h, wu_ref[...])).astype(BF16)


def ffn_up(h, w_gu, *, tm, tn):
    rows, d = h.shape
    hid = w_gu.shape[1] // 2
    nj = hid // tn
    return pl.pallas_call(
        _ffn_up_kernel,
        grid=(rows // tm, nj),
        in_specs=[pl.BlockSpec((tm, d), lambda i, j: (i, 0)),
                  pl.BlockSpec((d, tn), lambda i, j: (0, j)),
                  pl.BlockSpec((d, tn), lambda i, j: (0, j + nj))],
        out_specs=pl.BlockSpec((tm, tn), lambda i, j: (i, j)),
        out_shape=jax.ShapeDtypeStruct((rows, hid), BF16),
        compiler_params=_cparams("arbitrary", "arbitrary"),
        name="ffn_up",
    )(h, w_gu, w_gu)


def _final_norm_kernel(x_ref, w_ref, o_ref):
    x = x_ref[...]
    o_ref[...] = x * lax.rsqrt(jnp.mean(x * x, axis=-1, keepdims=True) + EPS) * w_ref[...]


def final_norm(x, w, *, tm):
    rows, d = x.shape
    return pl.pallas_call(
        _final_norm_kernel,
        grid=(rows // tm,),
        in_specs=[pl.BlockSpec((tm, d), lambda i: (i, 0)), pl.BlockSpec((1, d), lambda i: (0, 0))],
        out_specs=pl.BlockSpec((tm, d), lambda i: (i, 0)),
        out_shape=jax.ShapeDtypeStruct((rows, d), F32),
        compiler_params=_cparams("arbitrary"),
        name="final_norm",
    )(x, w)


def _sink_softmax(s, sink_col):
    m = jnp.maximum(jnp.max(s, axis=-1, keepdims=True), sink_col)
    p = jnp.exp(s - m)
    return p * (1.0 / (jnp.sum(p, axis=-1, keepdims=True) + jnp.exp(sink_col - m)))


def _sink_column(sink_ref, kv, rows):
    half = lax.broadcasted_iota(jnp.int32, (2 * rows, 1), 0) < rows
    return jnp.where(half, sink_ref[:, 2 * kv:2 * kv + 1], sink_ref[:, 2 * kv + 1:2 * kv + 2])


def _ctx_attn_kernel(q_ref, k_ref, v_ref, sink_ref, o_ref):
    t = q_ref.shape[0]
    scale = HEAD_DIM ** -0.5
    scores = {}
    for kv in range(A_KV):
        lo = 2 * kv * HEAD_DIM
        q2 = jnp.concatenate([q_ref[:, lo:lo + HEAD_DIM], q_ref[:, lo + HEAD_DIM:lo + 2 * HEAD_DIM]], axis=0)
        k = k_ref[:, kv * HEAD_DIM:(kv + 1) * HEAD_DIM].astype(BF16)
        scores[kv] = _dot_nt(q2.astype(BF16), k) * scale
    for kv in range(A_KV):
        lo = 2 * kv * HEAD_DIM
        v = v_ref[:, kv * HEAD_DIM:(kv + 1) * HEAD_DIM].astype(BF16)
        p = _sink_softmax(scores[kv], _sink_column(sink_ref, kv, t))
        o = _dot(p.astype(BF16), v)
        o_ref[:, lo:lo + HEAD_DIM] = o[:t].astype(BF16)
        o_ref[:, lo + HEAD_DIM:lo + 2 * HEAD_DIM] = o[t:].astype(BF16)


def context_attention(zb, zf, sink_row):
    batch, seq, _ = zb.shape
    return pl.pallas_call(
        _ctx_attn_kernel,
        grid=(batch,),
        in_specs=[pl.BlockSpec((None, seq, MIX_W), lambda b: (b, 0, Z_AQ // MIX_W)),
                  pl.BlockSpec((None, seq, KV_W), lambda b: (b, 0, Z_AK // KV_W)),
                  pl.BlockSpec((None, seq, KV_W), lambda b: (b, 0, Z_AV // KV_W)),
                  pl.BlockSpec((1, 128), lambda b: (0, 0))],
        out_specs=pl.BlockSpec((None, seq, MIX_W), lambda b: (b, 0, 0)),
        out_shape=jax.ShapeDtypeStruct((batch, seq, MIX_W), BF16),
        compiler_params=_cparams("arbitrary"),
        name="context_attention",
    )(zb, zf, zf, sink_row)


def _rope(x, cos, sin_signed):
    lane = lax.broadcasted_iota(jnp.int32, x.shape, 1)
    swapped = jnp.where(lane % 64 < 32, pltpu.roll(x, 96, 1), pltpu.roll(x, 32, 1))
    return x * cos + swapped * sin_signed


def _rope_kernel(q_ref, k_ref, cos_ref, sin_ref, qo_ref, ko_ref):
    cos, sin = cos_ref[...], sin_ref[...]
    for h in range(N_HEADS):
        sl = slice(h * HEAD_DIM, (h + 1) * HEAD_DIM)
        qo_ref[:, sl] = _rope(q_ref[:, sl].astype(F32), cos, sin).astype(BF16)
    for h in range(A_KV):
        sl = slice(h * HEAD_DIM, (h + 1) * HEAD_DIM)
        ko_ref[:, sl] = _rope(k_ref[:, sl], cos, sin).astype(BF16)


def rope_tables(seq):
    pos = np.arange(seq)
    quarter = HEAD_DIM // 4
    inv = jnp.asarray(ROPE_BASE, F32) ** (-jnp.arange(quarter, dtype=F32) / quarter)
    row = jnp.asarray(pos // GRID_W, F32)[:, None] * inv[None, :]
    col = jnp.asarray(pos % GRID_W, F32)[:, None] * inv[None, :]
    cos = jnp.concatenate([jnp.cos(row), jnp.cos(row), jnp.cos(col), jnp.cos(col)], axis=1)
    sin = jnp.concatenate([-jnp.sin(row), jnp.sin(row), -jnp.sin(col), jnp.sin(col)], axis=1)
    return cos, sin


def rope_qk(zb, zf, cos, sin, *, tb):
    batch, seq, _ = zb.shape
    return pl.pallas_call(
        _rope_kernel,
        grid=(batch, seq // tb),
        in_specs=[pl.BlockSpec((None, tb, MIX_W), lambda b, t: (b, t, Z_AQ // MIX_W)),
                  pl.BlockSpec((None, tb, KV_W), lambda b, t: (b, t, Z_AK // KV_W)),
                  pl.BlockSpec((tb, HEAD_DIM), lambda b, t: (t, 0)),
                  pl.BlockSpec((tb, HEAD_DIM), lambda b, t: (t, 0))],
        out_specs=[pl.BlockSpec((None, tb, MIX_W), lambda b, t: (b, t, 0)),
                   pl.BlockSpec((None, tb, KV_W), lambda b, t: (b, t, 0))],
        out_shape=[jax.ShapeDtypeStruct((batch, seq, MIX_W), BF16),
                   jax.ShapeDtypeStruct((batch, seq, KV_W), BF16)],
        compiler_params=_cparams("arbitrary", "arbitrary"),
        name="rope_qk",
    )(zb, zf, cos, sin)


def _lat_attn_kernel(q_ref, kp_ref, kc_ref, kn_ref, vp_ref, vc_ref, vn_ref, kx_ref, vx_ref, sink_ref,
                     o_ref, *, seq):
    nq = Q_BLOCK
    past = kx_ref.shape[0]
    scale = HEAD_DIM ** -0.5
    ncol = 3 * nq + past
    jj = lax.broadcasted_iota(jnp.int32, (2 * nq, ncol), 1)
    tt = lax.broadcasted_iota(jnp.int32, (2 * nq, ncol), 0) % nq
    in_band = jnp.abs(tt + nq - jj) <= WINDOW
    units = [(j, kv) for j in range(ATT_QBLOCKS) for kv in range(A_KV)]
    scores, ok = {}, {}
    for j in range(ATT_QBLOCKS):
        kpos = (pl.program_id(1) * ATT_QBLOCKS + j) * nq - nq + jj
        ok[j] = (jj >= 3 * nq) | (in_band & (kpos >= 0) & (kpos < seq))
    for j, kv in units:
        lo = 2 * kv * HEAD_DIM
        ks = slice(kv * HEAD_DIM, (kv + 1) * HEAD_DIM)
        rows = slice(j * nq, (j + 1) * nq)
        kblocks = [kp_ref[:, ks]] + [kc_ref[i * nq:(i + 1) * nq, ks] for i in range(ATT_QBLOCKS)] + [kn_ref[:, ks]]
        q2 = jnp.concatenate([q_ref[rows, lo:lo + HEAD_DIM], q_ref[rows, lo + HEAD_DIM:lo + 2 * HEAD_DIM]], axis=0)
        kcat = jnp.concatenate(kblocks[j:j + 3] + [kx_ref[:, ks].astype(BF16)], axis=0)
        scores[j, kv] = _dot_nt(q2, kcat)
    probs = {}
    for j, kv in units:
        s = jnp.where(ok[j], scores[j, kv] * scale, -jnp.inf)
        probs[j, kv] = _sink_softmax(s, _sink_column(sink_ref, kv, nq)).astype(BF16)
    for j, kv in units:
        lo = 2 * kv * HEAD_DIM
        ks = slice(kv * HEAD_DIM, (kv + 1) * HEAD_DIM)
        rows = slice(j * nq, (j + 1) * nq)
        vblocks = [vp_ref[:, ks]] + [vc_ref[i * nq:(i + 1) * nq, ks] for i in range(ATT_QBLOCKS)] + [vn_ref[:, ks]]
        vcat = jnp.concatenate([blk.astype(BF16) for blk in vblocks[j:j + 3]] + [vx_ref[:, ks].astype(BF16)], axis=0)
        o = _dot(probs[j, kv], vcat)
        o_ref[rows, lo:lo + HEAD_DIM] = o[:nq].astype(BF16)
        o_ref[rows, lo + HEAD_DIM:lo + 2 * HEAD_DIM] = o[nq:].astype(BF16)


def latent_attention(qr, kr, zf, cache_k, cache_v, sink_row, *, layer):
    batch, seq, _ = zf.shape
    nb = seq // Q_BLOCK
    past = cache_k.shape[2]
    wide = ATT_QBLOCKS * Q_BLOCK

    def prev(q):
        return jnp.maximum(q * ATT_QBLOCKS - 1, 0)

    def nxt(q):
        return jnp.minimum((q + 1) * ATT_QBLOCKS, nb - 1)

    vcol = Z_AV // KV_W
    return pl.pallas_call(
        functools.partial(_lat_attn_kernel, seq=seq),
        grid=(batch, nb // ATT_QBLOCKS),
        in_specs=[pl.BlockSpec((None, wide, MIX_W), lambda b, q: (b, q, 0)),
                  pl.BlockSpec((None, Q_BLOCK, KV_W), lambda b, q: (b, prev(q), 0)),
                  pl.BlockSpec((None, wide, KV_W), lambda b, q: (b, q, 0)),
                  pl.BlockSpec((None, Q_BLOCK, KV_W), lambda b, q: (b, nxt(q), 0)),
                  pl.BlockSpec((None, Q_BLOCK, KV_W), lambda b, q: (b, prev(q), vcol)),
                  pl.BlockSpec((None, wide, KV_W), lambda b, q: (b, q, vcol)),
                  pl.BlockSpec((None, Q_BLOCK, KV_W), lambda b, q: (b, nxt(q), vcol)),
                  pl.BlockSpec((None, None, past, KV_W), lambda b, q: (b, layer, 0, 0)),
                  pl.BlockSpec((None, None, past, KV_W), lambda b, q: (b, layer, 0, 0)),
                  pl.BlockSpec((1, 128), lambda b, q: (0, 0))],
        out_specs=pl.BlockSpec((None, wide, MIX_W), lambda b, q: (b, q, 0)),
        out_shape=jax.ShapeDtypeStruct((batch, seq, MIX_W), BF16),
        compiler_params=_cparams("arbitrary", "arbitrary"),
        name="latent_attention",
    )(qr, kr, kr, kr, zf, zf, zf, cache_k, cache_v, sink_row)


def _store_gated_head_norm(total, o_ref, g_ref, nw_ref, act):
    inv = {u: lax.rsqrt(jnp.mean(t * t, axis=-1, keepdims=True) + EPS) for u, t in total.items()}
    for (sb, h), t in total.items():
        sl = slice(h * HEAD_DIM, (h + 1) * HEAD_DIM)
        o_ref[sb, :, sl] = (act(g_ref[sb, :, sl]) * ((t * inv[sb, h]) * nw_ref[:, sl])).astype(BF16)


def _tri_mask(n, rev):
    t = lax.broadcasted_iota(jnp.int32, (n, n), 0)
    s = lax.broadcasted_iota(jnp.int32, (n, n), 1)
    return (s >= t) if rev else (s <= t)


def _chunk_index(rev, n):
    return (lambda c: n - 1 - c) if rev else (lambda c: c)


def _scan_specs(L, width, cidx):
    def zspec(col):
        return pl.BlockSpec((SCAN_SEQS, L, width), lambda b, c: (b, cidx(c), col // width))
    return zspec


def _state_spec(*dims):
    return pl.BlockSpec((SCAN_SEQS,) + dims, lambda b, c: (b,) + (0,) * len(dims))


def _mlstm_kernel(*refs, rev, d, combine):
    if combine:
        (q_ref, k_ref, v_ref, t_ref, bias_ref, c0_ref, n0_ref, m0_ref, hp_ref, g_ref, nw_ref,
         o_ref, c_ref, n_ref, m_ref, cn_ref) = refs
    else:
        (q_ref, k_ref, v_ref, t_ref, bias_ref, c0_ref, n0_ref, m0_ref,
         o_ref, c_ref, n_ref, m_ref, cn_ref) = refs
    L = SCAN_CHUNK
    DK = HEAD_DIM

    @pl.when(pl.program_id(1) == 0)
    def _():
        m_ref[...] = m0_ref[...]
        for sb in range(SCAN_SEQS):
            for h in range(N_HEADS):
                cn_ref[sb, h, :, :DK] = c0_ref[sb, h]
                cn_ref[sb, h, :, DK:] = jnp.broadcast_to(n0_ref[sb, h:h + 1, :], (DK, DK)).T

    tri = _tri_mask(L, rev)
    tri_b = tri.astype(BF16)
    last = 0 if rev else L - 1
    scale = HEAD_DIM ** -0.5
    gates, cum = {}, {}
    for sb in range(SCAN_SEQS):
        gates[sb] = t_ref[sb] + bias_ref[...]
        hi, mid, lo = _split3(_log_sigmoid(gates[sb]))
        cum[sb] = _dot(tri_b, hi) + _dot(tri_b, mid) + _dot(tri_b, lo)
    units = [(sb, h) for sb in range(SCAN_SEQS) for h in range(N_HEADS)]
    gates_t = {sb: gates[sb].T for sb in range(SCAN_SEQS)}
    cum_t = {sb: cum[sb].T for sb in range(SCAN_SEQS)}
    m_all = {sb: m_ref[sb] for sb in range(SCAN_SEQS)}
    ones = jnp.ones((L, DK), BF16)
    col, qb, ks, v1 = {}, {}, {}, {}
    for u in units:
        sb, h = u
        ci, cf = d * 8 + h, d * 8 + 4 + h
        sl = slice(h * HEAD_DIM, (h + 1) * HEAD_DIM)
        i_col, b_col = gates[sb][:, ci:ci + 1], cum[sb][:, cf:cf + 1]
        m = m_all[sb][:, h:h + 1]
        bl = b_col[last:last + 1, :]
        g = bl - b_col + i_col
        m_new = jnp.maximum(bl + m, jnp.max(g, axis=0, keepdims=True))
        col[u] = dict(m=m, b=b_col, m_new=m_new, w_old=jnp.exp(bl + m - m_new), e=jnp.exp(g - m_new))
        qb[u] = q_ref[sb, :, sl].astype(BF16)
        ks[u] = k_ref[sb, :, sl].astype(F32) * scale
        v1[u] = jnp.concatenate([v_ref[sb, :, sl].astype(BF16), ones], axis=1)
    kw_t = {u: (col[u]["e"] * ks[u]).T.astype(BF16) for u in units}
    s_raw, q_cn = {}, {}
    for u in units:
        sb, h = u
        cn_old = cn_ref[sb, h]
        s_raw[u] = _dot_nt(qb[u], ks[u].astype(BF16))
        q_cn[u] = _dot(qb[u], cn_old.astype(BF16))
        cn_ref[sb, h] = col[u]["w_old"] * cn_old + _dot(kw_t[u], v1[u])
    wmat = {}
    for u in units:
        sb, h = u
        ci, cf = d * 8 + h, d * 8 + 4 + h
        r_row = gates_t[sb][ci:ci + 1, :] - cum_t[sb][cf:cf + 1, :]
        wmat[u] = jnp.where(tri, r_row, -jnp.inf)
    cmax = {u: jnp.max(wmat[u], axis=1, keepdims=True) for u in units}
    u_bc, s = {}, {}
    for u in units:
        u_bc[u] = jnp.maximum(jnp.broadcast_to(col[u]["m"], (L, L)), jnp.broadcast_to(cmax[u], (L, L)))
        s[u] = (s_raw[u] * jnp.exp(wmat[u] - u_bc[u])).astype(BF16)
    sv1 = {u: _dot(s[u], v1[u]) for u in units}
    total = {}
    for u in units:
        sb, h = u
        sl = slice(h * HEAD_DIM, (h + 1) * HEAD_DIM)
        w_state = jnp.exp(col[u]["m"] - u_bc[u])
        den = w_state * q_cn[u][:, DK:] + sv1[u][:, DK:]
        floor = jnp.exp(-(jnp.broadcast_to(col[u]["b"], (L, L)) + u_bc[u]))
        hh = (w_state * q_cn[u][:, :DK] + sv1[u][:, :DK]) * (1.0 / jnp.maximum(jnp.abs(den), floor))
        if combine:
            total[u] = hh + hp_ref[sb, :, sl]
        else:
            o_ref[sb, :, sl] = hh
    if combine:
        _store_gated_head_norm(total, o_ref, g_ref, nw_ref, jax.nn.sigmoid)
    for u in units:
        sb, h = u
        m_ref[sb, :, h:h + 1] = col[u]["m_new"]

    @pl.when(pl.program_id(1) == pl.num_programs(1) - 1)
    def _():
        for sb in range(SCAN_SEQS):
            for h in range(N_HEADS):
                c_ref[sb, h] = cn_ref[sb, h, :, :DK]
                n_ref[sb, h:h + 1, :] = cn_ref[sb, h, :, DK:].T[0:1, :]


def mlstm_direction(zb, zf, bias_row, c0, n0, m0, *, d, prev=None, norm_w=None):
    batch, seq, _ = zb.shape
    L =SCAN_CHUNK
    nc = seq // L
    rev = d == 1
    cidx = _chunk_index(rev, nc)
    combine = prev is not None
    zspec = _scan_specs(L, MIX_W, cidx)
    state_specs = [_state_spec(N_HEADS, HEAD_DIM, HEAD_DIM), _state_spec(N_HEADS, HEAD_DIM),
                   _state_spec(1, N_HEADS)]
    in_specs = [zspec(Z_MQ), zspec(Z_MK), zspec(Z_MV),
                pl.BlockSpec((SCAN_SEQS, L, 128), lambda b, c: (b, cidx(c), Z_TAIL // 128)),
                pl.BlockSpec((1, 128), lambda b, c: (0, 0))] + state_specs
    args = [zb, zb, zb, zf, bias_row, c0, n0, m0]
    if combine:
        in_specs += [zspec(0), zspec(Z_MO), pl.BlockSpec((1, MIX_W), lambda b, c: (0, 0))]
        args += [prev, zf, norm_w]
    return pl.pallas_call(
        functools.partial(_mlstm_kernel, rev=rev, d=d, combine=combine),
        grid=(batch // SCAN_SEQS, nc),
        in_specs=in_specs,
        out_specs=[zspec(0)] + state_specs,
        out_shape=[jax.ShapeDtypeStruct((batch, seq, MIX_W), BF16 if combine else F32),
                   jax.ShapeDtypeStruct((batch, N_HEADS, HEAD_DIM, HEAD_DIM), F32),
                   jax.ShapeDtypeStruct((batch, N_HEADS, HEAD_DIM), F32),
                   jax.ShapeDtypeStruct((batch, 1, N_HEADS), F32)],
        scratch_shapes=[pltpu.VMEM((SCAN_SEQS, N_HEADS, HEAD_DIM, 2 * HEAD_DIM), F32)],
        compiler_params=_cparams("arbitrary", "arbitrary"),
        name="mlstm_scan",
    )(*args)


def _retention_kernel(*refs, rev, combine):
    if combine:
        q_ref, k_ref, v_ref, dec_ref, s0_ref, hp_ref, g_ref, nw_ref, o_ref, s_ref = refs
    else:
        q_ref, k_ref, v_ref, dec_ref, s0_ref, o_ref, s_ref = refs
    L = SCAN_CHUNK

    @pl.when(pl.program_id(1) == 0)
    def _():
        s_ref[...] = s0_ref[...]

    tri = _tri_mask(L, rev)
    t_i = lax.broadcasted_iota(jnp.int32, (L, L), 0)
    s_i = lax.broadcasted_iota(jnp.int32, (L, L), 1)
    dist = jnp.abs(t_i - s_i).astype(F32)
    pos = lax.broadcasted_iota(jnp.int32, (L, 1), 0).astype(F32)
    xi_pow = (L - pos) if rev else (pos + 1.0)
    zeta_pow = pos if rev else (L - 1.0 - pos)
    log_gamma = _log_sigmoid(dec_ref[...])
    scale = HEAD_DIM ** -0.5
    units = [(sb, h) for h in range(N_HEADS) for sb in range(SCAN_SEQS)]
    stage1 = {}
    for sb, h in units:
        sl = slice(h * HEAD_DIM, (h + 1) * HEAD_DIM)
        lg = log_gamma[:, h:h + 1]
        qb = q_ref[sb, :, sl].astype(BF16)
        ks = k_ref[sb, :, sl].astype(F32) * scale
        vb = v_ref[sb, :, sl].astype(BF16)
        s_old = s_ref[sb, h]
        att = _dot_nt(qb, ks.astype(BF16))
        inter = _dot(qb, s_old.astype(BF16))
        outer = _dot((ks * jnp.exp(zeta_pow * lg)).T.astype(BF16), vb)
        s_ref[sb, h] = jnp.exp(L * lg) * s_old + outer
        stage1[sb, h] = (att, inter, vb)
    decay, xi = {}, {}
    for h in range(N_HEADS):
        lg = log_gamma[:, h:h + 1]
        decay[h] = jnp.where(tri, jnp.exp(dist * lg), 0.0)
        xi[h] = jnp.exp(xi_pow * lg)
    total = {}
    for sb, h in units:
        sl = slice(h * HEAD_DIM, (h + 1) * HEAD_DIM)
        att, inter, vb = stage1[sb, h]
        y = _dot((att * decay[h]).astype(BF16), vb) + xi[h] * inter
        if combine:
            total[sb, h] = y + hp_ref[sb, :, sl]
        else:
            o_ref[sb, :, sl] = y
    if combine:
        _store_gated_head_norm(total, o_ref, g_ref, nw_ref, jax.nn.silu)


def retention_direction(zb, zf, decay_row, s0, *, d, prev=None, norm_w=None):
    batch, seq, _ = zb.shape
    L =SCAN_CHUNK
    nc = seq // L
    rev = d == 1
    cidx = _chunk_index(rev, nc)
    combine = prev is not None
    zspec = _scan_specs(L, MIX_W, cidx)
    state_spec = _state_spec(N_HEADS, HEAD_DIM, HEAD_DIM)
    in_specs = [zspec(Z_RQ), zspec(Z_RK), zspec(Z_RV), pl.BlockSpec((1, 128), lambda b, c: (0, 0)), state_spec]
    args = [zb, zb, zb, decay_row, s0]
    if combine:
        in_specs += [zspec(0), zspec(Z_RG), pl.BlockSpec((1, MIX_W), lambda b, c: (0, 0))]
        args += [prev, zf, norm_w]
    return pl.pallas_call(
        functools.partial(_retention_kernel, rev=rev, combine=combine),
        grid=(batch // SCAN_SEQS, nc),
        in_specs=in_specs,
        out_specs=[zspec(0), state_spec],
        out_shape=[jax.ShapeDtypeStruct((batch, seq, MIX_W), BF16 if combine else F32),
                   jax.ShapeDtypeStruct((batch, N_HEADS, HEAD_DIM, HEAD_DIM), F32)],
        compiler_params=_cparams("arbitrary", "arbitrary"),
        name="retention_scan",
    )(*args)


GLA_LEVELS = (32, 16, 8, 4, 2, 1)
GLA_LANES = N_HEADS * G_DK


def _gla_constants(rev):
    L = GLA_CHUNK
    t = np.arange(L)[:, None]
    r = np.arange(L)[None, :]
    mats = [(r <= t).astype(np.float32)]
    pair = [np.eye(L, dtype=np.float32)]
    for w in GLA_LEVELS:
        blk_t, blk_r = t // (2 * w), r // (2 * w)
        ref_t = blk_t * 2 * w + w - 1
        upper_t = (t % (2 * w)) >= w
        mats.append((upper_t & (r > ref_t) & (r <= t)).astype(np.float32))
        mats.append(((~upper_t) & (r > t) & (r <= ref_t)).astype(np.float32))
        lower_r = (r % (2 * w)) < w
        pair.append((upper_t & lower_r & (blk_t == blk_r)).astype(np.float32))
    if rev:
        mats = [m[::-1, ::-1] for m in mats]
        pair = [p[::-1, ::-1] for p in pair]
    big = np.concatenate(mats, axis=0)
    big3 = np.concatenate([big, big, big], axis=1)
    pair = np.stack([np.tile(p, (1, 2)) for p in pair])
    return jnp.asarray(big3, BF16), jnp.asarray(pair, F32)


def _gla_kernel(*refs, rev, combine):
    if combine:
        (q_ref, k_ref, v_ref, t_ref, w2_ref, gb_ref, big_ref, pair_ref, s0_ref,
         hp_ref, g_ref, nw_ref, o_ref, so_ref, s_ref) = refs
    else:
        (q_ref, k_ref, v_ref, t_ref, w2_ref, gb_ref, big_ref, pair_ref, s0_ref,
         o_ref, so_ref, s_ref) = refs
    L = GLA_CHUNK
    c = pl.program_id(1)
    pairs = N_HEADS // 2
    pw = 2 * G_DK
    ow = 2 * HEAD_DIM

    @pl.when(c == 0)
    def _():
        s_ref[...] = jnp.zeros_like(s_ref)
        for sb in range(SCAN_SEQS):
            for h in range(N_HEADS):
                hl = h % 2
                s_ref[sb, h // 2, hl * G_DK:(hl + 1) * G_DK, hl * HEAD_DIM:(hl + 1) * HEAD_DIM] = s0_ref[sb, h]

    last = 0 if rev else L - 1
    first_head = lax.broadcasted_iota(jnp.int32, (L, pw), 1) < G_DK
    diag_block = ((lax.broadcasted_iota(jnp.int32, (pw, ow), 0) < G_DK)
                  == (lax.broadcasted_iota(jnp.int32, (pw, ow), 1) < HEAD_DIM))
    zeros_v = jnp.zeros((L, HEAD_DIM), BF16)

    def stacked(x):
        return jnp.concatenate([jnp.where(first_head, x, 0.0), jnp.where(first_head, 0.0, x)], axis=0).astype(BF16)

    seqs = range(SCAN_SEQS)
    units = [(sb, p) for sb in seqs for p in range(pairs)]
    logit = {sb: _dot(t_ref[sb].astype(BF16), w2_ref[...]) + gb_ref[...] for sb in seqs}
    e = {}
    for sb in seqs:
        la = _log_sigmoid(logit[sb]) * (1.0 / GLA_TAU)
        e[sb] = _dot(big_ref[...], jnp.concatenate(_split3(la), axis=0))
    pad = jnp.zeros((128 - L, pw), F32)
    pad_v = jnp.zeros((128 - L, ow), BF16)
    stage = {}
    for sb, p in units:
        lanes = slice(p * pw, (p + 1) * pw)
        cum = e[sb][0:L, lanes]
        q = q_ref[sb, :, lanes].astype(F32) * (G_DK ** -0.5)
        k = k_ref[sb, :, lanes].astype(F32)
        vb = v_ref[sb, :, p * ow:(p + 1) * ow].astype(BF16)
        s_old = s_ref[sb, p]
        att = pair_ref[0] * _dot_nt(q.astype(BF16), stacked(k))
        for i in range(len(GLA_LEVELS)):
            eq = e[sb][(2 * i + 1) * L:(2 * i + 2) * L, lanes]
            ek = e[sb][(2 * i + 2) * L:(2 * i + 3) * L, lanes]
            att = att + pair_ref[i + 1] * _dot_nt((q * jnp.exp(eq)).astype(BF16), stacked(k * jnp.exp(ek)))
        inter = _dot((q * jnp.exp(cum)).astype(BF16), s_old.astype(BF16))
        last_row = cum[last:last + 1, :]
        kd_t = jnp.concatenate([k * jnp.exp(last_row - cum), pad], axis=0).T
        cum_t = jnp.concatenate([cum, pad], axis=0).T
        upd = _dot(kd_t.astype(BF16), jnp.concatenate([vb, pad_v], axis=0))
        s_ref[sb, p] = jnp.exp(cum_t[:, last:last + 1]) * s_old + jnp.where(diag_block, upd, 0.0)
        stage[sb, p] = (att.astype(BF16), inter, vb)
    total = {}
    for sb, p in units:
        att, inter, vb = stage[sb, p]
        vblk = jnp.concatenate([jnp.concatenate([vb[:, :HEAD_DIM], zeros_v], axis=1),
                                jnp.concatenate([zeros_v, vb[:, HEAD_DIM:]], axis=1)], axis=0)
        y = _dot(att, vblk) + inter
        for hl in range(2):
            h = 2 * p + hl
            sl = slice(h * HEAD_DIM, (h + 1) * HEAD_DIM)
            yh = y[:, hl * HEAD_DIM:(hl + 1) * HEAD_DIM]
            if combine:
                total[sb, h] = yh + hp_ref[sb, :, sl]
            else:
                o_ref[sb, :, sl] = yh
    if combine:
        _store_gated_head_norm(total, o_ref, g_ref, nw_ref, jax.nn.silu)

    @pl.when(c == pl.num_programs(1) - 1)
    def _():
        for sb in range(SCAN_SEQS):
            for h in range(N_HEADS):
                hl = h % 2
                so_ref[sb, h] = s_ref[sb, h // 2, hl * G_DK:(hl + 1) * G_DK, hl * HEAD_DIM:(hl + 1) * HEAD_DIM]


def gla_direction(zb, zf, w2pad, gbias, s0, *, d, prev=None, norm_w=None):
    batch, seq, _ = zb.shape
    L =GLA_CHUNK
    nc = seq // L
    rev = d == 1
    cidx = _chunk_index(rev, nc)
    combine = prev is not None
    big3, pair = _gla_constants(rev)

    def const(a):
        return pl.BlockSpec(a.shape, lambda b, c: (0,) * a.ndim)

    zq = _scan_specs(L, GLA_LANES, cidx)
    zv = _scan_specs(L, MIX_W, cidx)
    state_spec = _state_spec(N_HEADS, G_DK, HEAD_DIM)
    in_specs = [zq(Z_GQ), zq(Z_GK), zv(Z_GV),
                pl.BlockSpec((SCAN_SEQS, L, 128), lambda b, c: (b, cidx(c), Z_TAIL // 128)),
                const(w2pad), const(gbias), const(big3), const(pair), state_spec]
    args = [zb, zb, zb, zf, w2pad, gbias, big3, pair, s0]
    if combine:
        in_specs += [zv(0), zv(Z_GG), pl.BlockSpec((1, MIX_W), lambda b, c: (0, 0))]
        args += [prev, zf, norm_w]
    return pl.pallas_call(
        functools.partial(_gla_kernel, rev=rev, combine=combine),
        grid=(batch // SCAN_SEQS, nc),
        in_specs=in_specs,
        out_specs=[zv(0), state_spec],
        out_shape=[jax.ShapeDtypeStruct((batch, seq, MIX_W), BF16 if combine else F32),
                   jax.ShapeDtypeStruct((batch, N_HEADS, G_DK, HEAD_DIM), F32)],
        scratch_shapes=[pltpu.VMEM((SCAN_SEQS, N_HEADS // 2, 2 * G_DK, 2 * HEAD_DIM), F32)],
        compiler_params=_cparams("arbitrary", "arbitrary"),
        name="gla_scan",
    )(*args)


def _pad_row(v, width=128):
    v = v.reshape(1, -1).astype(F32)
    return jnp.pad(v, ((0, 0), (0, width - v.shape[1])))


def _permute_w_in(w_in_l):
    sizes = (512, 256, 256, 512, 512, 512, 512, 16, 512, 512, 512, 512, 256, 256, 512, 512, 32)
    offs = np.concatenate([[0], np.cumsum(sizes)])
    pieces = [w_in_l[:, offs[i]:offs[i + 1]] for i in range(len(sizes))]
    order = [0, 3, 4, 5, 8, 9, 10, 12, 13, 14,
             1, 2, 6, 11, 15, 7, 16]
    w = jnp.concatenate([pieces[i] for i in order], axis=1)
    return jnp.pad(w, ((0, 0), (0, ZB_COLS + ZF_COLS - w.shape[1]))).astype(BF16)


def _prepare_layer(p, l):
    d = p["w_in"].shape[1]
    w2pad = []
    for direction in range(2):
        lo = TAIL_GLR + direction * G_RANK
        w2pad.append(jnp.zeros((128, GLA_LANES), F32).at[lo:lo + G_RANK].set(p["gla_w2"][l, direction]).astype(BF16))
    return {
        "w_in": _permute_w_in(p["w_in"][l]),
        "w_mgate": p["w_mgate"][l].astype(BF16), "w_br": p["w_br"][l].astype(BF16),
        "w_out": p["w_out"][l].astype(BF16), "ffn_w_gu": p["ffn_w_gu"][l].astype(BF16),
        "ffn_w_down": p["ffn_w_down"][l].astype(BF16),
        "norm1_w": p["norm1_w"][l].reshape(1, d), "norm2_w": p["norm2_w"][l].reshape(1, d),
        "sink_row": _pad_row(p["attn_sink"][l]), "mlstm_bias": _pad_row(p["mlstm_if_b"][l]),
        "mlstm_norm_w": p["mlstm_norm_w"][l].reshape(1, MIX_W),
        "ret_decay": [_pad_row(p["ret_decay"][l, 0]), _pad_row(p["ret_decay"][l, 1])],
        "ret_norm_w": p["ret_norm_w"][l].reshape(1, MIX_W),
        "gla_w2": w2pad, "gla_b": [p["gla_b"][l, 0].reshape(1, GLA_LANES), p["gla_b"][l, 1].reshape(1, GLA_LANES)],
        "gla_norm_w": p["gla_norm_w"][l].reshape(1, MIX_W),
    }


def _mixers(zb, zf, w, l, cache):
    batch, seq, _ = zb.shape
    is_ctx = cache is None
    if is_ctx:
        ya = context_attention(zb, zf, w["sink_row"])
    else:
        cos, sin = rope_tables(seq)
        qr, kr = rope_qk(zb, zf, cos, sin, tb=min(seq, 512))
        ck = cache["attn_k"].reshape(cache["attn_k"].shape[:3] + (KV_W,))
        cv = cache["attn_v"].reshape(cache["attn_v"].shape[:3] + (KV_W,))
        ya = latent_attention(qr, kr, zf, ck, cv, w["sink_row"], layer=l)

    def zeros(*dims):
        return jnp.zeros((batch,) + dims, F32)

    res = {}
    for d in (1, 0):
        if is_ctx:
            init = (zeros(N_HEADS, HEAD_DIM, HEAD_DIM), zeros(N_HEADS, HEAD_DIM), zeros(1, N_HEADS))
        else:
            init = (cache["mlstm_C"][:, l, d], cache["mlstm_n"][:, l, d],
                    cache["mlstm_m"][:, l, d].reshape(batch, 1, N_HEADS))
        res[d] = mlstm_direction(zb, zf, w["mlstm_bias"], *init, d=d, prev=res[1][0] if d == 0 else None,
                                 norm_w=w["mlstm_norm_w"] if d == 0 else None)
    ym = res[0][0]
    states = {"mlstm_C": jnp.stack([res[0][1], res[1][1]], axis=1),
              "mlstm_n": jnp.stack([res[0][2], res[1][2]], axis=1),
              "mlstm_m": jnp.stack([res[0][3][:, 0], res[1][3][:, 0]], axis=1)}

    res = {}
    for d in (1, 0):
        s0 = zeros(N_HEADS, HEAD_DIM, HEAD_DIM) if is_ctx else cache["ret_S"][:, l, d]
        res[d] = retention_direction(zb, zf, w["ret_decay"][d], s0, d=d, prev=res[1][0] if d == 0 else None,
                                     norm_w=w["ret_norm_w"] if d == 0 else None)
    yr = res[0][0]
    states["ret_S"] = jnp.stack([res[0][1], res[1][1]], axis=1)

    res = {}
    for d in (1, 0):
        s0 = zeros(N_HEADS, G_DK, HEAD_DIM) if is_ctx else cache["gla_S"][:, l, d]
        res[d] = gla_direction(zb, zf, w["gla_w2"][d], w["gla_b"][d], s0, d=d, prev=res[1][0] if d == 0 else None,
                               norm_w=w["gla_norm_w"] if d == 0 else None)
    yg = res[0][0]
    states["gla_S"] = jnp.stack([res[0][1], res[1][1]], axis=1)

    rows = batch * seq
    ys = [y.reshape(rows, MIX_W) for y in (ya, ym, yr, yg)]
    states["attn_k"] = zf[:, :, Z_AK:Z_AK + KV_W].reshape(batch, seq, A_KV, HEAD_DIM)
    states["attn_v"] = zf[:, :, Z_AV:Z_AV + KV_W].reshape(batch, seq, A_KV, HEAD_DIM)
    return ys, states


def _layer(x, dims, mod_l, w, l, cache, *, grp, tm):
    batch, seq = dims
    d = x.shape[1]
    sh1, sc1, g1, sh2, sc2, g2 = (mod_l[:, i * d:(i + 1) * d] for i in range(6))
    zb, zf, h = in_projection(x, sh1, sc1, w["norm1_w"], w["w_in"], grp=grp, tm=tm)
    ys, states = _mixers(zb.reshape(batch, seq, ZB_COLS), zf.reshape(batch, seq, ZF_COLS), w, l, cache)
    merged = merge_branches(h, ys, w["w_mgate"], w["w_br"], tm=tm, tn=256)
    x, h2 = out_projection_norm(merged, w["w_out"], x, g1, sh2, sc2, w["norm2_w"], grp=grp, tm=tm // 2)
    act = ffn_up(h2, w["ffn_w_gu"], tm=tm, tn=512)
    x = residual_projection(act, w["ffn_w_down"], x, g2, grp=grp, tm=tm, tn=512)
    return x, states


def _forward(x_prompt, x_sample, cache, c, c_ctx, p, final_norm_w, *, tm):
    cb, ct, d = x_prompt.shape
    lb, lt, _ = x_sample.shape
    assert (cb * ct) % tm == 0 and lt % tm == 0 and lb + 1 <= MOD_ROWS
    assert cb % SCAN_SEQS == 0 and lb % SCAN_SEQS == 0
    depth = p["w_ada"].shape[0]
    cond = jnp.concatenate([c_ctx[None, :], c, jnp.zeros((MOD_ROWS - 1 - lb, d), F32)], axis=0)
    mod = ada_modulation(cond, p["w_ada"], p["b_ada"])
    xc = x_prompt.reshape(cb * ct, d)
    xl = x_sample.reshape(lb * lt, d)
    ctx_states = []
    for l in range(depth):
        w = _prepare_layer(p, l)
        xc, st = _layer(xc, (cb, ct), mod[l], w, l, None, grp=(0, cb * ct), tm=tm)
        xl, _ = _layer(xl, (lb, lt), mod[l], w, l, cache, grp=(1, lt), tm=tm)
        ctx_states.append(st)
    fw = final_norm_w.reshape(1, d)
    y_prompt = final_norm(xc, fw, tm=tm).reshape(cb, ct, d)
    y_sample = final_norm(xl, fw, tm=tm).reshape(lb, lt, d)

    def stack(name):
        return jnp.stack([s[name] for s in ctx_states], axis=1)

    return (y_prompt, y_sample, stack("attn_k"), stack("attn_v"), stack("mlstm_C"), stack("mlstm_n"),
            stack("mlstm_m"), stack("ret_S"), stack("gla_S"))


def kernel(x_prompt, x_sample, cache_attn_k, cache_attn_v, state_mlstm_C, state_mlstm_n, state_mlstm_m, state_ret_S, state_gla_S, c, c_ctx, w_ada, b_ada, norm1_w, norm2_w, w_in, attn_sink, mlstm_if_b, mlstm_norm_w, ret_decay, ret_norm_w, gla_w2, gla_b, gla_norm_w, w_br, w_mgate, w_out, ffn_w_gu, ffn_w_down, final_norm_w):
    p = {"w_ada": w_ada, "b_ada": b_ada, "norm1_w": norm1_w, "norm2_w": norm2_w, "w_in": w_in,
         "attn_sink": attn_sink, "mlstm_if_b": mlstm_if_b, "mlstm_norm_w": mlstm_norm_w,
         "ret_decay": ret_decay, "ret_norm_w": ret_norm_w, "gla_w2": gla_w2, "gla_b": gla_b,
         "gla_norm_w": gla_norm_w, "w_br": w_br, "w_mgate": w_mgate, "w_out": w_out,
         "ffn_w_gu": ffn_w_gu, "ffn_w_down": ffn_w_down}
    cache = {"attn_k": cache_attn_k, "attn_v": cache_attn_v, "mlstm_C": state_mlstm_C,
             "mlstm_n": state_mlstm_n, "mlstm_m": state_mlstm_m, "ret_S": state_ret_S, "gla_S": state_gla_S}
    return _forward(x_prompt, x_sample, cache, c, c_ctx, p, final_norm_w, tm=1024)
```

```python
import functools

import numpy as np
import jax
import jax.numpy as jnp
from jax import lax
from jax.experimental import pallas as pl
from jax.experimental.pallas import tpu as pltpu

F32 = jnp.float32
BF16 = jnp.bfloat16

HEAD_DIM = 128
N_HEADS = 4
A_KV = 2
GRID_W = 64
WINDOW = 128
Q_BLOCK = 128
G_DK = 64
G_RANK = 16
GLA_TAU = 16.0
ROPE_BASE = 10000.0
EPS = 1e-6
MOD_ROWS = 16
SCAN_CHUNK = 128
GLA_CHUNK = 64
SCAN_SEQS = 8
ATT_QBLOCKS = 4
NORM_ROWS = 64
MIX_W = N_HEADS * HEAD_DIM
KV_W = A_KV * HEAD_DIM

Z_AQ = 0
Z_MQ, Z_MK, Z_MV = 512, 1024, 1536
Z_RQ, Z_RK, Z_RV = 2048, 2560, 3072
Z_GQ, Z_GK, Z_GV = 3584, 3840, 4096
ZB_COLS = 4608
Z_AK, Z_AV = 0, 256
Z_MO, Z_RG, Z_GG = 512, 1024, 1536
Z_TAIL = 2048
ZF_COLS = 2304
Z_TN = 1152
TAIL_GLR = 16

VMEM_LIMIT = 56 * 2 ** 20


def _cparams(*sem):
    return pltpu.CompilerParams(dimension_semantics=sem, vmem_limit_bytes=VMEM_LIMIT)


def _log_sigmoid(x):
    return jnp.minimum(x, 0.0) - jnp.log1p(jnp.exp(-jnp.abs(x)))


def _split3(x):
    hi = x.astype(BF16)
    r = x - hi.astype(F32)
    mid = r.astype(BF16)
    lo = (r - mid.astype(F32)).astype(BF16)
    return hi, mid, lo


def _dot(a, b):
    return jnp.dot(a, b, preferred_element_type=F32)


def _dot_nt(a, b):
    return lax.dot_general(a, b, (((1,), (1,)), ((), ())), preferred_element_type=F32)


def _ada_kernel(cond_ref, w_ref, b_ref, o_ref):
    s = jax.nn.silu(cond_ref[...]).astype(BF16)
    o_ref[0] = _dot(s, w_ref[0].astype(BF16)) + b_ref[0]


def ada_modulation(cond, w_ada, b_ada):
    depth, d, n = w_ada.shape
    tn = 1024
    return pl.pallas_call(
        _ada_kernel,
        grid=(depth, n // tn),
        in_specs=[pl.BlockSpec((MOD_ROWS, d), lambda l, j: (0, 0)),
                  pl.BlockSpec((1, d, tn), lambda l, j: (l, 0, j)),
                  pl.BlockSpec((1, 1, tn), lambda l, j: (l, 0, j))],
        out_specs=pl.BlockSpec((1, MOD_ROWS, tn), lambda l, j: (l, 0, j)),
        out_shape=jax.ShapeDtypeStruct((depth, MOD_ROWS, n), F32),
        compiler_params=_cparams("arbitrary", "arbitrary"),
        name="ada_modulation",
    )(cond, w_ada, b_ada.reshape(depth, 1, n))


def _mod_row(grp, tm):
    return grp[0] + (pl.program_id(0) * tm) // grp[1]


def _modulated_norm(x, nw, sh, sc):
    y = x * lax.rsqrt(jnp.mean(x * x, axis=-1, keepdims=True) + EPS)
    return (y * nw) * (1.0 + sc) + sh


def _in_proj_kernel(x_ref, sh_ref, sc_ref, nw_ref, w_ref, zb_ref, zf_ref, h_ref, *, tm, grp, nb):
    j = pl.program_id(1)

    @pl.when(j == 0)
    def _():
        g = _mod_row(grp, tm)
        nw, sh, sc = nw_ref[...], sh_ref[pl.ds(g, 1), :], sc_ref[pl.ds(g, 1), :]
        for r in range(0, tm, NORM_ROWS):
            rows = slice(r, r + NORM_ROWS)
            h_ref[rows, :] = _modulated_norm(x_ref[rows, :], nw, sh, sc).astype(BF16)

    @pl.when(j < nb)
    def _():
        zb_ref[...] = _dot(h_ref[...], w_ref[...]).astype(BF16)

    @pl.when(j >= nb)
    def _():
        zf_ref[...] = _dot(h_ref[...], w_ref[...])


def in_projection(x, sh, sc, nw, w, *, grp, tm):
    rows, d = x.shape
    nb, nf = ZB_COLS // Z_TN, ZF_COLS // Z_TN
    return pl.pallas_call(
        functools.partial(_in_proj_kernel, tm=tm, grp=grp, nb=nb),
        grid=(rows // tm, nb + nf),
        in_specs=[pl.BlockSpec((tm, d), lambda i, j: (i, 0)),
                  pl.BlockSpec((MOD_ROWS, d), lambda i, j: (0, 0)),
                  pl.BlockSpec((MOD_ROWS, d), lambda i, j: (0, 0)),
                  pl.BlockSpec((1, d), lambda i, j: (0, 0)),
                  pl.BlockSpec((d, Z_TN), lambda i, j: (0, j))],
        out_specs=[pl.BlockSpec((tm, Z_TN), lambda i, j: (i, jnp.minimum(j, nb - 1))),
                   pl.BlockSpec((tm, Z_TN), lambda i, j: (i, jnp.maximum(j - nb, 0))),
                   pl.BlockSpec((tm, d), lambda i, j: (i, 0))],
        out_shape=[jax.ShapeDtypeStruct((rows, ZB_COLS), BF16), jax.ShapeDtypeStruct((rows, ZF_COLS), F32),
                   jax.ShapeDtypeStruct((rows, d), BF16)],
        compiler_params=_cparams("arbitrary", "arbitrary"),
        name="in_projection",
    )(x, sh, sc, nw, w)


def _merge_kernel(h_ref, ya_ref, ym_ref, yr_ref, yg_ref, wg_ref, wb_ref, o_ref):
    h = h_ref[...]
    acc = None
    for b, y_ref in enumerate((ya_ref, ym_ref, yr_ref, yg_ref)):
        gate = jax.nn.sigmoid(_dot(h, wg_ref[b]))
        term = gate * _dot(y_ref[...], wb_ref[b])
        acc = term if acc is None else acc + term
    o_ref[...] = acc.astype(BF16)


def merge_branches(h, ys, w_mgate, w_br, *, tm, tn):
    rows, d = h.shape
    nb, mw, _ = w_br.shape
    return pl.pallas_call(
        _merge_kernel,
        grid=(rows // tm, d // tn),
        in_specs=[pl.BlockSpec((tm, d), lambda i, j: (i, 0))]
        + [pl.BlockSpec((tm, mw), lambda i, j: (i, 0))] * nb
        + [pl.BlockSpec((nb, d, tn), lambda i, j: (0, 0, j)),
           pl.BlockSpec((nb, mw, tn), lambda i, j: (0, 0, j))],
        out_specs=pl.BlockSpec((tm, tn), lambda i, j: (i, j)),
        out_shape=jax.ShapeDtypeStruct((rows, d), BF16),
        compiler_params=_cparams("arbitrary", "arbitrary"),
        name="merge_branches",
    )(h, *ys, w_mgate, w_br)


def _residual_proj_kernel(a_ref, w_ref, x_ref, g_ref, o_ref, *, tm, grp):
    g = _mod_row(grp, tm)
    o_ref[...] = x_ref[...] + g_ref[pl.ds(g, 1), :] * _dot(a_ref[...], w_ref[...])


def residual_projection(a, w, x, gate, *, grp, tm, tn):
    rows, k = a.shape
    n = w.shape[1]
    return pl.pallas_call(
        functools.partial(_residual_proj_kernel, tm=tm, grp=grp),
        grid=(rows // tm, n // tn),
        in_specs=[pl.BlockSpec((tm, k), lambda i, j: (i, 0)),
                  pl.BlockSpec((k, tn), lambda i, j: (0, j)),
                  pl.BlockSpec((tm, tn), lambda i, j: (i, j)),
                  pl.BlockSpec((MOD_ROWS, tn), lambda i, j: (0, j))],
        out_specs=pl.BlockSpec((tm, tn), lambda i, j: (i, j)),
        out_shape=jax.ShapeDtypeStruct((rows, n), F32),
        compiler_params=_cparams("arbitrary", "arbitrary"),
        name="residual_projection",
    )(a, w, x, gate)


def _out_proj_norm_kernel(a_ref, w_ref, x_ref, g_ref, sh_ref, sc_ref, nw_ref, o_ref, h_ref, *, tm, grp):
    g = _mod_row(grp, tm)
    x = x_ref[...] + g_ref[pl.ds(g, 1), :] * _dot(a_ref[...], w_ref[...])
    o_ref[...] = x
    h_ref[...] = _modulated_norm(x, nw_ref[...], sh_ref[pl.ds(g, 1), :], sc_ref[pl.ds(g, 1), :]).astype(BF16)


def out_projection_norm(a, w, x, gate, sh, sc, nw, *, grp, tm):
    rows, k = a.shape
    d = w.shape[1]
    full = lambda i: (0, 0)
    return pl.pallas_call(
        functools.partial(_out_proj_norm_kernel, tm=tm, grp=grp),
        grid=(rows // tm,),
        in_specs=[pl.BlockSpec((tm, k), lambda i: (i, 0)),
                  pl.BlockSpec((k, d), full),
                  pl.BlockSpec((tm, d), lambda i: (i, 0)),
                  pl.BlockSpec((MOD_ROWS, d), full),
                  pl.BlockSpec((MOD_ROWS, d), full),
                  pl.BlockSpec((MOD_ROWS, d), full),
                  pl.BlockSpec((1, d), full)],
        out_specs=[pl.BlockSpec((tm, d), lambda i: (i, 0)), pl.BlockSpec((tm, d), lambda i: (i, 0))],
        out_shape=[jax.ShapeDtypeStruct((rows, d), F32), jax.ShapeDtypeStruct((rows, d), BF16)],
        compiler_params=_cparams("arbitrary"),
        name="out_projection_norm",
    )(a, w, x, gate, sh, sc, nw)


def _ffn_up_kernel(h_ref, wg_ref, wu_ref, o_ref):
    h = h_ref[...]
    o_ref[...] = (jax.nn.silu(_dot(h, wg_ref[...])) * _dot(h, wu_ref[...])).astype(BF16)


def ffn_up(h, w_gu, *, tm, tn):
    rows, d = h.shape
    hid = w_gu.shape[1] // 2
    nj = hid // tn
    return pl.pallas_call(
        _ffn_up_kernel,
        grid=(rows // tm, nj),
        in_specs=[pl.BlockSpec((tm, d), lambda i, j: (i, 0)),
                  pl.BlockSpec((d, tn), lambda i, j: (0, j)),
                  pl.BlockSpec((d, tn), lambda i, j: (0, j + nj))],
        out_specs=pl.BlockSpec((tm, tn), lambda i, j: (i, j)),
        out_shape=jax.ShapeDtypeStruct((rows, hid), BF16),
        compiler_params=_cparams("arbitrary", "arbitrary"),
        name="ffn_up",
    )(h, w_gu, w_gu)


def _final_norm_kernel(x_ref, w_ref, o_ref):
    x = x_ref[...]
    o_ref[...] = x * lax.rsqrt(jnp.mean(x * x, axis=-1, keepdims=True) + EPS) * w_ref[...]


def final_norm(x, w, *, tm):
    rows, d = x.shape
    return pl.pallas_call(
        _final_norm_kernel,
        grid=(rows // tm,),
        in_specs=[pl.BlockSpec((tm, d), lambda i: (i, 0)), pl.BlockSpec((1, d), lambda i: (0, 0))],
        out_specs=pl.BlockSpec((tm, d), lambda i: (i, 0)),
        out_shape=jax.ShapeDtypeStruct((rows, d), F32),
        compiler_params=_cparams("arbitrary"),
        name="final_norm",
    )(x, w)


def _sink_softmax(s, sink_col):
    m = jnp.maximum(jnp.max(s, axis=-1, keepdims=True), sink_col)
    p = jnp.exp(s - m)
    return p * (1.0 / (jnp.sum(p, axis=-1, keepdims=True) + jnp.exp(sink_col - m)))


def _sink_column(sink_ref, kv, rows):
    half = lax.broadcasted_iota(jnp.int32, (2 * rows, 1), 0) < rows
    return jnp.where(half, sink_ref[:, 2 * kv:2 * kv + 1], sink_ref[:, 2 * kv + 1:2 * kv + 2])


def _ctx_attn_kernel(q_ref, k_ref, v_ref, sink_ref, o_ref):
    t = q_ref.shape[0]
    scale = HEAD_DIM ** -0.5
    scores = {}
    for kv in range(A_KV):
        lo = 2 * kv * HEAD_DIM
        q2 = jnp.concatenate([q_ref[:, lo:lo + HEAD_DIM], q_ref[:, lo + HEAD_DIM:lo + 2 * HEAD_DIM]], axis=0)
        k = k_ref[:, kv * HEAD_DIM:(kv + 1) * HEAD_DIM].astype(BF16)
        scores[kv] = _dot_nt(q2.astype(BF16), k) * scale
    for kv in range(A_KV):
        lo = 2 * kv * HEAD_DIM
        v = v_ref[:, kv * HEAD_DIM:(kv + 1) * HEAD_DIM].astype(BF16)
        p = _sink_softmax(scores[kv], _sink_column(sink_ref, kv, t))
        o = _dot(p.astype(BF16), v)
        o_ref[:, lo:lo + HEAD_DIM] = o[:t].astype(BF16)
        o_ref[:, lo + HEAD_DIM:lo + 2 * HEAD_DIM] = o[t:].astype(BF16)


def context_attention(zb, zf, sink_row):
    batch, seq, _ = zb.shape
    return pl.pallas_call(
        _ctx_attn_kernel,
        grid=(batch,),
        in_specs=[pl.BlockSpec((None, seq, MIX_W), lambda b: (b, 0, Z_AQ // MIX_W)),
                  pl.BlockSpec((None, seq, KV_W), lambda b: (b, 0, Z_AK // KV_W)),
                  pl.BlockSpec((None, seq, KV_W), lambda b: (b, 0, Z_AV // KV_W)),
                  pl.BlockSpec((1, 128), lambda b: (0, 0))],
        out_specs=pl.BlockSpec((None, seq, MIX_W), lambda b: (b, 0, 0)),
        out_shape=jax.ShapeDtypeStruct((batch, seq, MIX_W), BF16),
        compiler_params=_cparams("arbitrary"),
        name="context_attention",
    )(zb, zf, zf, sink_row)


def _rope(x, cos, sin_signed):
    lane = lax.broadcasted_iota(jnp.int32, x.shape, 1)
    swapped = jnp.where(lane % 64 < 32, pltpu.roll(x, 96, 1), pltpu.roll(x, 32, 1))
    return x * cos + swapped * sin_signed


def _rope_kernel(q_ref, k_ref, cos_ref, sin_ref, qo_ref, ko_ref):
    cos, sin = cos_ref[...], sin_ref[...]
    for h in range(N_HEADS):
        sl = slice(h * HEAD_DIM, (h + 1) * HEAD_DIM)
        qo_ref[:, sl] = _rope(q_ref[:, sl].astype(F32), cos, sin).astype(BF16)
    for h in range(A_KV):
        sl = slice(h * HEAD_DIM, (h + 1) * HEAD_DIM)
        ko_ref[:, sl] = _rope(k_ref[:, sl], cos, sin).astype(BF16)


def rope_tables(seq):
    pos = np.arange(seq)
    quarter = HEAD_DIM // 4
    inv = jnp.asarray(ROPE_BASE, F32) ** (-jnp.arange(quarter, dtype=F32) / quarter)
    row = jnp.asarray(pos // GRID_W, F32)[:, None] * inv[None, :]
    col = jnp.asarray(pos % GRID_W, F32)[:, None] * inv[None, :]
    cos = jnp.concatenate([jnp.cos(row), jnp.cos(row), jnp.cos(col), jnp.cos(col)], axis=1)
    sin = jnp.concatenate([-jnp.sin(row), jnp.sin(row), -jnp.sin(col), jnp.sin(col)], axis=1)
    return cos, sin


def rope_qk(zb, zf, cos, sin, *, tb):
    batch, seq, _ = zb.shape
    return pl.pallas_call(
        _rope_kernel,
        grid=(batch, seq // tb),
        in_specs=[pl.BlockSpec((None, tb, MIX_W), lambda b, t: (b, t, Z_AQ // MIX_W)),
                  pl.BlockSpec((None, tb, KV_W), lambda b, t: (b, t, Z_AK // KV_W)),
                  pl.BlockSpec((tb, HEAD_DIM), lambda b, t: (t, 0)),
                  pl.BlockSpec((tb, HEAD_DIM), lambda b, t: (t, 0))],
        out_specs=[pl.BlockSpec((None, tb, MIX_W), lambda b, t: (b, t, 0)),
                   pl.BlockSpec((None, tb, KV_W), lambda b, t: (b, t, 0))],
        out_shape=[jax.ShapeDtypeStruct((batch, seq, MIX_W), BF16),
                   jax.ShapeDtypeStruct((batch, seq, KV_W), BF16)],
        compiler_params=_cparams("arbitrary", "arbitrary"),
        name="rope_qk",
    )(zb, zf, cos, sin)


def _lat_attn_kernel(q_ref, kp_ref, kc_ref, kn_ref, vp_ref, vc_ref, vn_ref, kx_ref, vx_ref, sink_ref,
                     o_ref, *, seq):
    nq = Q_BLOCK
    past = kx_ref.shape[0]
    scale = HEAD_DIM ** -0.5
    ncol = 3 * nq + past
    jj = lax.broadcasted_iota(jnp.int32, (2 * nq, ncol), 1)
    tt = lax.broadcasted_iota(jnp.int32, (2 * nq, ncol), 0) % nq
    in_band = jnp.abs(tt + nq - jj) <= WINDOW
    units = [(j, kv) for j in range(ATT_QBLOCKS) for kv in range(A_KV)]
    scores, ok = {}, {}
    for j in range(ATT_QBLOCKS):
        kpos = (pl.program_id(1) * ATT_QBLOCKS + j) * nq - nq + jj
        ok[j] = (jj >= 3 * nq) | (in_band & (kpos >= 0) & (kpos < seq))
    for j, kv in units:
        lo = 2 * kv * HEAD_DIM
        ks = slice(kv * HEAD_DIM, (kv + 1) * HEAD_DIM)
        rows = slice(j * nq, (j + 1) * nq)
        kblocks = [kp_ref[:, ks]] + [kc_ref[i * nq:(i + 1) * nq, ks] for i in range(ATT_QBLOCKS)] + [kn_ref[:, ks]]
        q2 = jnp.concatenate([q_ref[rows, lo:lo + HEAD_DIM], q_ref[rows, lo + HEAD_DIM:lo + 2 * HEAD_DIM]], axis=0)
        kcat = jnp.concatenate(kblocks[j:j + 3] + [kx_ref[:, ks].astype(BF16)], axis=0)
        scores[j, kv] = _dot_nt(q2, kcat)
    probs = {}
    for j, kv in units:
        s = jnp.where(ok[j], scores[j, kv] * scale, -jnp.inf)
        probs[j, kv] = _sink_softmax(s, _sink_column(sink_ref, kv, nq)).astype(BF16)
    for j, kv in units:
        lo = 2 * kv * HEAD_DIM
        ks = slice(kv * HEAD_DIM, (kv + 1) * HEAD_DIM)
        rows = slice(j * nq, (j + 1) * nq)
        vblocks = [vp_ref[:, ks]] + [vc_ref[i * nq:(i + 1) * nq, ks] for i in range(ATT_QBLOCKS)] + [vn_ref[:, ks]]
        vcat = jnp.concatenate([blk.astype(BF16) for blk in vblocks[j:j + 3]] + [vx_ref[:, ks].astype(BF16)], axis=0)
        o = _dot(probs[j, kv], vcat)
        o_ref[rows, lo:lo + HEAD_DIM] = o[:nq].astype(BF16)
        o_ref[rows, lo + HEAD_DIM:lo + 2 * HEAD_DIM] = o[nq:].astype(BF16)


def latent_attention(qr, kr, zf, cache_k, cache_v, sink_row, *, layer):
    batch, seq, _ = zf.shape
    nb = seq // Q_BLOCK
    past = cache_k.shape[2]
    wide = ATT_QBLOCKS * Q_BLOCK

    def prev(q):
        return jnp.maximum(q * ATT_QBLOCKS - 1, 0)

    def nxt(q):
        return jnp.minimum((q + 1) * ATT_QBLOCKS, nb - 1)

    vcol = Z_AV // KV_W
    return pl.pallas_call(
        functools.partial(_lat_attn_kernel, seq=seq),
        grid=(batch, nb // ATT_QBLOCKS),
        in_specs=[pl.BlockSpec((None, wide, MIX_W), lambda b, q: (b, q, 0)),
                  pl.BlockSpec((None, Q_BLOCK, KV_W), lambda b, q: (b, prev(q), 0)),
                  pl.BlockSpec((None, wide, KV_W), lambda b, q: (b, q, 0)),
                  pl.BlockSpec((None, Q_BLOCK, KV_W), lambda b, q: (b, nxt(q), 0)),
                  pl.BlockSpec((None, Q_BLOCK, KV_W), lambda b, q: (b, prev(q), vcol)),
                  pl.BlockSpec((None, wide, KV_W), lambda b, q: (b, q, vcol)),
                  pl.BlockSpec((None, Q_BLOCK, KV_W), lambda b, q: (b, nxt(q), vcol)),
                  pl.BlockSpec((None, None, past, KV_W), lambda b, q: (b, layer, 0, 0)),
                  pl.BlockSpec((None, None, past, KV_W), lambda b, q: (b, layer, 0, 0)),
                  pl.BlockSpec((1, 128), lambda b, q: (0, 0))],
        out_specs=pl.BlockSpec((None, wide, MIX_W), lambda b, q: (b, q, 0)),
        out_shape=jax.ShapeDtypeStruct((batch, seq, MIX_W), BF16),
        compiler_params=_cparams("arbitrary", "arbitrary"),
        name="latent_attention",
    )(qr, kr, kr, kr, zf, zf, zf, cache_k, cache_v, sink_row)


def _store_gated_head_norm(total, o_ref, g_ref, nw_ref, act):
    inv = {u: lax.rsqrt(jnp.mean(t * t, axis=-1, keepdims=True) + EPS) for u, t in total.items()}
    for (sb, h), t in total.items():
        sl = slice(h * HEAD_DIM, (h + 1) * HEAD_DIM)
        o_ref[sb, :, sl] = (act(g_ref[sb, :, sl]) * ((t * inv[sb, h]) * nw_ref[:, sl])).astype(BF16)


def _tri_mask(n, rev):
    t = lax.broadcasted_iota(jnp.int32, (n, n), 0)
    s = lax.broadcasted_iota(jnp.int32, (n, n), 1)
    return (s >= t) if rev else (s <= t)


def _chunk_index(rev, n):
    return (lambda c: n - 1 - c) if rev else (lambda c: c)


def _scan_specs(L, width, cidx):
    def zspec(col):
        return pl.BlockSpec((SCAN_SEQS, L, width), lambda b, c: (b, cidx(c), col // width))
    return zspec


def _state_spec(*dims):
    return pl.BlockSpec((SCAN_SEQS,) + dims, lambda b, c: (b,) + (0,) * len(dims))


def _mlstm_kernel(*refs, rev, d, combine):
    if combine:
        (q_ref, k_ref, v_ref, t_ref, bias_ref, c0_ref, n0_ref, m0_ref, hp_ref, g_ref, nw_ref,
         o_ref, c_ref, n_ref, m_ref, cn_ref) = refs
    else:
        (q_ref, k_ref, v_ref, t_ref, bias_ref, c0_ref, n0_ref, m0_ref,
         o_ref, c_ref, n_ref, m_ref, cn_ref) = refs
    L = SCAN_CHUNK
    DK = HEAD_DIM

    @pl.when(pl.program_id(1) == 0)
    def _():
        m_ref[...] = m0_ref[...]
        for sb in range(SCAN_SEQS):
            for h in range(N_HEADS):
                cn_ref[sb, h, :, :DK] = c0_ref[sb, h]
                cn_ref[sb, h, :, DK:] = jnp.broadcast_to(n0_ref[sb, h:h + 1, :], (DK, DK)).T

    tri = _tri_mask(L, rev)
    tri_b = tri.astype(BF16)
    last = 0 if rev else L - 1
    scale = HEAD_DIM ** -0.5
    gates, cum = {}, {}
    for sb in range(SCAN_SEQS):
        gates[sb] = t_ref[sb] + bias_ref[...]
        hi, mid, lo = _split3(_log_sigmoid(gates[sb]))
        cum[sb] = _dot(tri_b, hi) + _dot(tri_b, mid) + _dot(tri_b, lo)
    units = [(sb, h) for sb in range(SCAN_SEQS) for h in range(N_HEADS)]
    gates_t = {sb: gates[sb].T for sb in range(SCAN_SEQS)}
    cum_t = {sb: cum[sb].T for sb in range(SCAN_SEQS)}
    m_all = {sb: m_ref[sb] for sb in range(SCAN_SEQS)}
    ones = jnp.ones((L, DK), BF16)
    col, qb, ks, v1 = {}, {}, {}, {}
    for u in units:
        sb, h = u
        ci, cf = d * 8 + h, d * 8 + 4 + h
        sl = slice(h * HEAD_DIM, (h + 1) * HEAD_DIM)
        i_col, b_col = gates[sb][:, ci:ci + 1], cum[sb][:, cf:cf + 1]
        m = m_all[sb][:, h:h + 1]
        bl = b_col[last:last + 1, :]
        g = bl - b_col + i_col
        m_new = jnp.maximum(bl + m, jnp.max(g, axis=0, keepdims=True))
        col[u] = dict(m=m, b=b_col, m_new=m_new, w_old=jnp.exp(bl + m - m_new), e=jnp.exp(g - m_new))
        qb[u] = q_ref[sb, :, sl].astype(BF16)
        ks[u] = k_ref[sb, :, sl].astype(F32) * scale
        v1[u] = jnp.concatenate([v_ref[sb, :, sl].astype(BF16), ones], axis=1)
    kw_t = {u: (col[u]["e"] * ks[u]).T.astype(BF16) for u in units}
    s_raw, q_cn = {}, {}
    for u in units:
        sb, h = u
        cn_old = cn_ref[sb, h]
        s_raw[u] = _dot_nt(qb[u], ks[u].astype(BF16))
        q_cn[u] = _dot(qb[u], cn_old.astype(BF16))
        cn_ref[sb, h] = col[u]["w_old"] * cn_old + _dot(kw_t[u], v1[u])
    wmat = {}
    for u in units:
        sb, h = u
        ci, cf = d * 8 + h, d * 8 + 4 + h
        r_row = gates_t[sb][ci:ci + 1, :] - cum_t[sb][cf:cf + 1, :]
        wmat[u] = jnp.where(tri, r_row, -jnp.inf)
    cmax = {u: jnp.max(wmat[u], axis=1, keepdims=True) for u in units}
    u_bc, s = {}, {}
    for u in units:
        u_bc[u] = jnp.maximum(jnp.broadcast_to(col[u]["m"], (L, L)), jnp.broadcast_to(cmax[u], (L, L)))
        s[u] = (s_raw[u] * jnp.exp(wmat[u] - u_bc[u])).astype(BF16)
    sv1 = {u: _dot(s[u], v1[u]) for u in units}
    total = {}
    for u in units:
        sb, h = u
        sl = slice(h * HEAD_DIM, (h + 1) * HEAD_DIM)
        w_state = jnp.exp(col[u]["m"] - u_bc[u])
        den = w_state * q_cn[u][:, DK:] + sv1[u][:, DK:]
        floor = jnp.exp(-(jnp.broadcast_to(col[u]["b"], (L, L)) + u_bc[u]))
        hh = (w_state * q_cn[u][:, :DK] + sv1[u][:, :DK]) * (1.0 / jnp.maximum(jnp.abs(den), floor))
        if combine:
            total[u] = hh + hp_ref[sb, :, sl]
        else:
            o_ref[sb, :, sl] = hh
    if combine:
        _store_gated_head_norm(total, o_ref, g_ref, nw_ref, jax.nn.sigmoid)
    for u in units:
        sb, h = u
        m_ref[sb, :, h:h + 1] = col[u]["m_new"]

    @pl.when(pl.program_id(1) == pl.num_programs(1) - 1)
    def _():
        for sb in range(SCAN_SEQS):
            for h in range(N_HEADS):
                c_ref[sb, h] = cn_ref[sb, h, :, :DK]
                n_ref[sb, h:h + 1, :] = cn_ref[sb, h, :, DK:].T[0:1, :]


def mlstm_direction(zb, zf, bias_row, c0, n0, m0, *, d, prev=None, norm_w=None):
    batch, seq, _ = zb.shape
    L =SCAN_CHUNK
    nc = seq // L
    rev = d == 1
    cidx = _chunk_index(rev, nc)
    combine = prev is not None
    zspec = _scan_specs(L, MIX_W, cidx)
    state_specs = [_state_spec(N_HEADS, HEAD_DIM, HEAD_DIM), _state_spec(N_HEADS, HEAD_DIM),
                   _state_spec(1, N_HEADS)]
    in_specs = [zspec(Z_MQ), zspec(Z_MK), zspec(Z_MV),
                pl.BlockSpec((SCAN_SEQS, L, 128), lambda b, c: (b, cidx(c), Z_TAIL // 128)),
                pl.BlockSpec((1, 128), lambda b, c: (0, 0))] + state_specs
    args = [zb, zb, zb, zf, bias_row, c0, n0, m0]
    if combine:
        in_specs += [zspec(0), zspec(Z_MO), pl.BlockSpec((1, MIX_W), lambda b, c: (0, 0))]
        args += [prev, zf, norm_w]
    return pl.pallas_call(
        functools.partial(_mlstm_kernel, rev=rev, d=d, combine=combine),
        grid=(batch // SCAN_SEQS, nc),
        in_specs=in_specs,
        out_specs=[zspec(0)] + state_specs,
        out_shape=[jax.ShapeDtypeStruct((batch, seq, MIX_W), BF16 if combine else F32),
                   jax.ShapeDtypeStruct((batch, N_HEADS, HEAD_DIM, HEAD_DIM), F32),
                   jax.ShapeDtypeStruct((batch, N_HEADS, HEAD_DIM), F32),
                   jax.ShapeDtypeStruct((batch, 1, N_HEADS), F32)],
        scratch_shapes=[pltpu.VMEM((SCAN_SEQS, N_HEADS, HEAD_DIM, 2 * HEAD_DIM), F32)],
        compiler_params=_cparams("arbitrary", "arbitrary"),
        name="mlstm_scan",
    )(*args)


def _retention_kernel(*refs, rev, combine):
    if combine:
        q_ref, k_ref, v_ref, dec_ref, s0_ref, hp_ref, g_ref, nw_ref, o_ref, s_ref = refs
    else:
        q_ref, k_ref, v_ref, dec_ref, s0_ref, o_ref, s_ref = refs
    L = SCAN_CHUNK

    @pl.when(pl.program_id(1) == 0)
    def _():
        s_ref[...] = s0_ref[...]

    tri = _tri_mask(L, rev)
    t_i = lax.broadcasted_iota(jnp.int32, (L, L), 0)
    s_i = lax.broadcasted_iota(jnp.int32, (L, L), 1)
    dist = jnp.abs(t_i - s_i).astype(F32)
    pos = lax.broadcasted_iota(jnp.int32, (L, 1), 0).astype(F32)
    xi_pow = (L - pos) if rev else (pos + 1.0)
    zeta_pow = pos if rev else (L - 1.0 - pos)
    log_gamma = _log_sigmoid(dec_ref[...])
    scale = HEAD_DIM ** -0.5
    units = [(sb, h) for h in range(N_HEADS) for sb in range(SCAN_SEQS)]
    stage1 = {}
    for sb, h in units:
        sl = slice(h * HEAD_DIM, (h + 1) * HEAD_DIM)
        lg = log_gamma[:, h:h + 1]
        qb = q_ref[sb, :, sl].astype(BF16)
        ks = k_ref[sb, :, sl].astype(F32) * scale
        vb = v_ref[sb, :, sl].astype(BF16)
        s_old = s_ref[sb, h]
        att = _dot_nt(qb, ks.astype(BF16))
        inter = _dot(qb, s_old.astype(BF16))
        outer = _dot((ks * jnp.exp(zeta_pow * lg)).T.astype(BF16), vb)
        s_ref[sb, h] = jnp.exp(L * lg) * s_old + outer
        stage1[sb, h] = (att, inter, vb)
    decay, xi = {}, {}
    for h in range(N_HEADS):
        lg = log_gamma[:, h:h + 1]
        decay[h] = jnp.where(tri, jnp.exp(dist * lg), 0.0)
        xi[h] = jnp.exp(xi_pow * lg)
    total = {}
    for sb, h in units:
        sl = slice(h * HEAD_DIM, (h + 1) * HEAD_DIM)
        att, inter, vb = stage1[sb, h]
        y = _dot((att * decay[h]).astype(BF16), vb) + xi[h] * inter
        if combine:
            total[sb, h] = y + hp_ref[sb, :, sl]
        else:
            o_ref[sb, :, sl] = y
    if combine:
        _store_gated_head_norm(total, o_ref, g_ref, nw_ref, jax.nn.silu)


def retention_direction(zb, zf, decay_row, s0, *, d, prev=None, norm_w=None):
    batch, seq, _ = zb.shape
    L =SCAN_CHUNK
    nc = seq // L
    rev = d == 1
    cidx = _chunk_index(rev, nc)
    combine = prev is not None
    zspec = _scan_specs(L, MIX_W, cidx)
    state_spec = _state_spec(N_HEADS, HEAD_DIM, HEAD_DIM)
    in_specs = [zspec(Z_RQ), zspec(Z_RK), zspec(Z_RV), pl.BlockSpec((1, 128), lambda b, c: (0, 0)), state_spec]
    args = [zb, zb, zb, decay_row, s0]
    if combine:
        in_specs += [zspec(0), zspec(Z_RG), pl.BlockSpec((1, MIX_W), lambda b, c: (0, 0))]
        args += [prev, zf, norm_w]
    return pl.pallas_call(
        functools.partial(_retention_kernel, rev=rev, combine=combine),
        grid=(batch // SCAN_SEQS, nc),
        in_specs=in_specs,
        out_specs=[zspec(0), state_spec],
        out_shape=[jax.ShapeDtypeStruct((batch, seq, MIX_W), BF16 if combine else F32),
                   jax.ShapeDtypeStruct((batch, N_HEADS, HEAD_DIM, HEAD_DIM), F32)],
        compiler_params=_cparams("arbitrary", "arbitrary"),
        name="retention_scan",
    )(*args)


GLA_LEVELS = (32, 16, 8, 4, 2, 1)
GLA_LANES = N_HEADS * G_DK


def _gla_constants(rev):
    L = GLA_CHUNK
    t = np.arange(L)[:, None]
    r = np.arange(L)[None, :]
    mats = [(r <= t).astype(np.float32)]
    pair = [np.eye(L, dtype=np.float32)]
    for w in GLA_LEVELS:
        blk_t, blk_r = t // (2 * w), r // (2 * w)
        ref_t = blk_t * 2 * w + w - 1
        upper_t = (t % (2 * w)) >= w
        mats.append(((upper_t & (r > ref_t) & (r <= t)) | ((~upper_t) & (r > t) & (r <= ref_t))).astype(np.float32))
        lower_r = (r % (2 * w)) < w
        pair.append((upper_t & lower_r & (blk_t == blk_r)).astype(np.float32))
    if rev:
        mats = [m[::-1, ::-1] for m in mats]
        pair = [p[::-1, ::-1] for p in pair]
    big = np.concatenate(mats, axis=0)
    big3 = np.concatenate([big, big, big], axis=1)
    pair = np.stack([np.tile(p, (1, 2)) for p in pair])
    return jnp.asarray(big3, BF16), jnp.asarray(pair, F32)


def _gla_kernel(*refs, rev, combine):
    if combine:
        (q_ref, k_ref, v_ref, t_ref, w2_ref, gb_ref, big_ref, pair_ref, s0_ref,
         hp_ref, g_ref, nw_ref, o_ref, so_ref, s_ref) = refs
    else:
        (q_ref, k_ref, v_ref, t_ref, w2_ref, gb_ref, big_ref, pair_ref, s0_ref,
         o_ref, so_ref, s_ref) = refs
    L = GLA_CHUNK
    c = pl.program_id(1)
    pairs = N_HEADS // 2
    pw = 2 * G_DK
    ow = 2 * HEAD_DIM

    @pl.when(c == 0)
    def _():
        s_ref[...] = jnp.zeros_like(s_ref)
        for sb in range(SCAN_SEQS):
            for h in range(N_HEADS):
                hl = h % 2
                s_ref[sb, h // 2, hl * G_DK:(hl + 1) * G_DK, hl * HEAD_DIM:(hl + 1) * HEAD_DIM] = s0_ref[sb, h]

    last = 0 if rev else L - 1
    first_head = lax.broadcasted_iota(jnp.int32, (L, pw), 1) < G_DK
    diag_block = ((lax.broadcasted_iota(jnp.int32, (pw, ow), 0) < G_DK)
                  == (lax.broadcasted_iota(jnp.int32, (pw, ow), 1) < HEAD_DIM))
    zeros_v = jnp.zeros((L, HEAD_DIM), BF16)

    def stacked(x):
        return jnp.concatenate([jnp.where(first_head, x, 0.0), jnp.where(first_head, 0.0, x)], axis=0).astype(BF16)

    seqs = range(SCAN_SEQS)
    units = [(sb, p) for sb in seqs for p in range(pairs)]
    logit = {sb: _dot(t_ref[sb].astype(BF16), w2_ref[...]) + gb_ref[...] for sb in seqs}
    e = {}
    for sb in seqs:
        la = _log_sigmoid(logit[sb]) * (1.0 / GLA_TAU)
        e[sb] = _dot(big_ref[...], jnp.concatenate(_split3(la), axis=0))
    pad = jnp.zeros((128 - L, pw), F32)
    pad_v = jnp.zeros((128 - L, ow), BF16)
    stage = {}
    for sb, p in units:
        lanes = slice(p * pw, (p + 1) * pw)
        cum = e[sb][0:L, lanes]
        q = q_ref[sb, :, lanes].astype(F32) * (G_DK ** -0.5)
        k = k_ref[sb, :, lanes].astype(F32)
        vb = v_ref[sb, :, p * ow:(p + 1) * ow].astype(BF16)
        s_old = s_ref[sb, p]
        att = pair_ref[0] * _dot_nt(q.astype(BF16), stacked(k))
        for i in range(len(GLA_LEVELS)):
            x = jnp.exp(e[sb][(i + 1) * L:(i + 2) * L, lanes])
            att = att + pair_ref[i + 1] * _dot_nt((q * x).astype(BF16), stacked(k * x))
        inter = _dot((q * jnp.exp(cum)).astype(BF16), s_old.astype(BF16))
        last_row = cum[last:last + 1, :]
        kd_t = jnp.concatenate([k * jnp.exp(last_row - cum), pad], axis=0).T
        cum_t = jnp.concatenate([cum, pad], axis=0).T
        upd = _dot(kd_t.astype(BF16), jnp.concatenate([vb, pad_v], axis=0))
        s_ref[sb, p] = jnp.exp(cum_t[:, last:last + 1]) * s_old + jnp.where(diag_block, upd, 0.0)
        stage[sb, p] = (att.astype(BF16), inter, vb)
    total = {}
    for sb, p in units:
        att, inter, vb = stage[sb, p]
        vblk = jnp.concatenate([jnp.concatenate([vb[:, :HEAD_DIM], zeros_v], axis=1),
                                jnp.concatenate([zeros_v, vb[:, HEAD_DIM:]], axis=1)], axis=0)
        y = _dot(att, vblk) + inter
        for hl in range(2):
            h = 2 * p + hl
            sl = slice(h * HEAD_DIM, (h + 1) * HEAD_DIM)
            yh = y[:, hl * HEAD_DIM:(hl + 1) * HEAD_DIM]
            if combine:
                total[sb, h] = yh + hp_ref[sb, :, sl]
            else:
                o_ref[sb, :, sl] = yh
    if combine:
        _store_gated_head_norm(total, o_ref, g_ref, nw_ref, jax.nn.silu)

    @pl.when(c == pl.num_programs(1) - 1)
    def _():
        for sb in range(SCAN_SEQS):
            for h in range(N_HEADS):
                hl = h % 2
                so_ref[sb, h] = s_ref[sb, h // 2, hl * G_DK:(hl + 1) * G_DK, hl * HEAD_DIM:(hl + 1) * HEAD_DIM]


def gla_direction(zb, zf, w2pad, gbias, s0, *, d, prev=None, norm_w=None):
    batch, seq, _ = zb.shape
    L =GLA_CHUNK
    nc = seq // L
    rev = d == 1
    cidx = _chunk_index(rev, nc)
    combine = prev is not None
    big3, pair = _gla_constants(rev)

    def const(a):
        return pl.BlockSpec(a.shape, lambda b, c: (0,) * a.ndim)

    zq = _scan_specs(L, GLA_LANES, cidx)
    zv = _scan_specs(L, MIX_W, cidx)
    state_spec = _state_spec(N_HEADS, G_DK, HEAD_DIM)
    in_specs = [zq(Z_GQ), zq(Z_GK), zv(Z_GV),
                pl.BlockSpec((SCAN_SEQS, L, 128), lambda b, c: (b, cidx(c), Z_TAIL // 128)),
                const(w2pad), const(gbias), const(big3), const(pair), state_spec]
    args = [zb, zb, zb, zf, w2pad, gbias, big3, pair, s0]
    if combine:
        in_specs += [zv(0), zv(Z_GG), pl.BlockSpec((1, MIX_W), lambda b, c: (0, 0))]
        args += [prev, zf, norm_w]
    return pl.pallas_call(
        functools.partial(_gla_kernel, rev=rev, combine=combine),
        grid=(batch // SCAN_SEQS, nc),
        in_specs=in_specs,
        out_specs=[zv(0), state_spec],
        out_shape=[jax.ShapeDtypeStruct((batch, seq, MIX_W), BF16 if combine else F32),
                   jax.ShapeDtypeStruct((batch, N_HEADS, G_DK, HEAD_DIM), F32)],
        scratch_shapes=[pltpu.VMEM((SCAN_SEQS, N_HEADS // 2, 2 * G_DK, 2 * HEAD_DIM), F32)],
        compiler_params=_cparams("arbitrary", "arbitrary"),
        name="gla_scan",
    )(*args)


def _pad_row(v, width=128):
    v = v.reshape(1, -1).astype(F32)
    return jnp.pad(v, ((0, 0), (0, width - v.shape[1])))


def _permute_w_in(w_in_l):
    sizes = (512, 256, 256, 512, 512, 512, 512, 16, 512, 512, 512, 512, 256, 256, 512, 512, 32)
    offs = np.concatenate([[0], np.cumsum(sizes)])
    pieces = [w_in_l[:, offs[i]:offs[i + 1]] for i in range(len(sizes))]
    order = [0, 3, 4, 5, 8, 9, 10, 12, 13, 14,
             1, 2, 6, 11, 15, 7, 16]
    w = jnp.concatenate([pieces[i] for i in order], axis=1)
    return jnp.pad(w, ((0, 0), (0, ZB_COLS + ZF_COLS - w.shape[1]))).astype(BF16)


def _prepare_layer(p, l):
    d = p["w_in"].shape[1]
    w2pad = []
    for direction in range(2):
        lo = TAIL_GLR + direction * G_RANK
        w2pad.append(jnp.zeros((128, GLA_LANES), F32).at[lo:lo + G_RANK].set(p["gla_w2"][l, direction]).astype(BF16))
    return {
        "w_in": _permute_w_in(p["w_in"][l]),
        "w_mgate": p["w_mgate"][l].astype(BF16), "w_br": p["w_br"][l].astype(BF16),
        "w_out": p["w_out"][l].astype(BF16), "ffn_w_gu": p["ffn_w_gu"][l].astype(BF16),
        "ffn_w_down": p["ffn_w_down"][l].astype(BF16),
        "norm1_w": p["norm1_w"][l].reshape(1, d), "norm2_w": p["norm2_w"][l].reshape(1, d),
        "sink_row": _pad_row(p["attn_sink"][l]), "mlstm_bias": _pad_row(p["mlstm_if_b"][l]),
        "mlstm_norm_w": p["mlstm_norm_w"][l].reshape(1, MIX_W),
        "ret_decay": [_pad_row(p["ret_decay"][l, 0]), _pad_row(p["ret_decay"][l, 1])],
        "ret_norm_w": p["ret_norm_w"][l].reshape(1, MIX_W),
        "gla_w2": w2pad, "gla_b": [p["gla_b"][l, 0].reshape(1, GLA_LANES), p["gla_b"][l, 1].reshape(1, GLA_LANES)],
        "gla_norm_w": p["gla_norm_w"][l].reshape(1, MIX_W),
    }


def _mixers(zb, zf, w, l, cache):
    batch, seq, _ = zb.shape
    is_ctx = cache is None
    if is_ctx:
        ya = context_attention(zb, zf, w["sink_row"])
    else:
        cos, sin = rope_tables(seq)
        qr, kr = rope_qk(zb, zf, cos, sin, tb=min(seq, 512))
        ck = cache["attn_k"].reshape(cache["attn_k"].shape[:3] + (KV_W,))
        cv = cache["attn_v"].reshape(cache["attn_v"].shape[:3] + (KV_W,))
        ya = latent_attention(qr, kr, zf, ck, cv, w["sink_row"], layer=l)

    def zeros(*dims):
        return jnp.zeros((batch,) + dims, F32)

    res = {}
    for d in (1, 0):
        if is_ctx:
            init = (zeros(N_HEADS, HEAD_DIM, HEAD_DIM), zeros(N_HEADS, HEAD_DIM), zeros(1, N_HEADS))
        else:
            init = (cache["mlstm_C"][:, l, d], cache["mlstm_n"][:, l, d],
                    cache["mlstm_m"][:, l, d].reshape(batch, 1, N_HEADS))
        res[d] = mlstm_direction(zb, zf, w["mlstm_bias"], *init, d=d, prev=res[1][0] if d == 0 else None,
                                 norm_w=w["mlstm_norm_w"] if d == 0 else None)
    ym = res[0][0]
    states = {"mlstm_C": jnp.stack([res[0][1], res[1][1]], axis=1),
              "mlstm_n": jnp.stack([res[0][2], res[1][2]], axis=1),
              "mlstm_m": jnp.stack([res[0][3][:, 0], res[1][3][:, 0]], axis=1)}

    res = {}
    for d in (1, 0):
        s0 = zeros(N_HEADS, HEAD_DIM, HEAD_DIM) if is_ctx else cache["ret_S"][:, l, d]
        res[d] = retention_direction(zb, zf, w["ret_decay"][d], s0, d=d, prev=res[1][0] if d == 0 else None,
                                     norm_w=w["ret_norm_w"] if d == 0 else None)
    yr = res[0][0]
    states["ret_S"] = jnp.stack([res[0][1], res[1][1]], axis=1)

    res = {}
    for d in (1, 0):
        s0 = zeros(N_HEADS, G_DK, HEAD_DIM) if is_ctx else cache["gla_S"][:, l, d]
        res[d] = gla_direction(zb, zf, w["gla_w2"][d], w["gla_b"][d], s0, d=d, prev=res[1][0] if d == 0 else None,
                               norm_w=w["gla_norm_w"] if d == 0 else None)
    yg = res[0][0]
    states["gla_S"] = jnp.stack([res[0][1], res[1][1]], axis=1)

    rows = batch * seq
    ys = [y.reshape(rows, MIX_W) for y in (ya, ym, yr, yg)]
    states["attn_k"] = zf[:, :, Z_AK:Z_AK + KV_W].reshape(batch, seq, A_KV, HEAD_DIM)
    states["attn_v"] = zf[:, :, Z_AV:Z_AV + KV_W].reshape(batch, seq, A_KV, HEAD_DIM)
    return ys, states


def _layer(x, dims, mod_l, w, l, cache, *, grp, tm):
    batch, seq = dims
    d = x.shape[1]
    sh1, sc1, g1, sh2, sc2, g2 = (mod_l[:, i * d:(i + 1) * d] for i in range(6))
    zb, zf, h = in_projection(x, sh1, sc1, w["norm1_w"], w["w_in"], grp=grp, tm=tm)
    ys, states = _mixers(zb.reshape(batch, seq, ZB_COLS), zf.reshape(batch, seq, ZF_COLS), w, l, cache)
    merged = merge_branches(h, ys, w["w_mgate"], w["w_br"], tm=tm, tn=256)
    x, h2 = out_projection_norm(merged, w["w_out"], x, g1, sh2, sc2, w["norm2_w"], grp=grp, tm=tm // 2)
    act = ffn_up(h2, w["ffn_w_gu"], tm=tm, tn=512)
    x = residual_projection(act, w["ffn_w_down"], x, g2, grp=grp, tm=tm, tn=512)
    return x, states


def _forward(x_prompt, x_sample, cache, c, c_ctx, p, final_norm_w, *, tm):
    cb, ct, d = x_prompt.shape
    lb, lt, _ = x_sample.shape
    assert (cb * ct) % tm == 0 and lt % tm == 0 and lb + 1 <= MOD_ROWS
    assert cb % SCAN_SEQS == 0 and lb % SCAN_SEQS == 0
    depth = p["w_ada"].shape[0]
    cond = jnp.concatenate([c_ctx[None, :], c, jnp.zeros((MOD_ROWS - 1 - lb, d), F32)], axis=0)
    mod = ada_modulation(cond, p["w_ada"], p["b_ada"])
    xc = x_prompt.reshape(cb * ct, d)
    xl = x_sample.reshape(lb * lt, d)
    ctx_states = []
    for l in range(depth):
        w = _prepare_layer(p, l)
        xc, st = _layer(xc, (cb, ct), mod[l], w, l, None, grp=(0, cb * ct), tm=tm)
        xl, _ = _layer(xl, (lb, lt), mod[l], w, l, cache, grp=(1, lt), tm=tm)
        ctx_states.append(st)
    fw = final_norm_w.reshape(1, d)
    y_prompt = final_norm(xc, fw, tm=tm).reshape(cb, ct, d)
    y_sample = final_norm(xl, fw, tm=tm).reshape(lb, lt, d)

    def stack(name):
        return jnp.stack([s[name] for s in ctx_states], axis=1)

    return (y_prompt, y_sample, stack("attn_k"), stack("attn_v"), stack("mlstm_C"), stack("mlstm_n"),
            stack("mlstm_m"), stack("ret_S"), stack("gla_S"))


def kernel(x_prompt, x_sample, cache_attn_k, cache_attn_v, state_mlstm_C, state_mlstm_n, state_mlstm_m, state_ret_S, state_gla_S, c, c_ctx, w_ada, b_ada, norm1_w, norm2_w, w_in, attn_sink, mlstm_if_b, mlstm_norm_w, ret_decay, ret_norm_w, gla_w2, gla_b, gla_norm_w, w_br, w_mgate, w_out, ffn_w_gu, ffn_w_down, final_norm_w):
    p = {"w_ada": w_ada, "b_ada": b_ada, "norm1_w": norm1_w, "norm2_w": norm2_w, "w_in": w_in,
         "attn_sink": attn_sink, "mlstm_if_b": mlstm_if_b, "mlstm_norm_w": mlstm_norm_w,
         "ret_decay": ret_decay, "ret_norm_w": ret_norm_w, "gla_w2": gla_w2, "gla_b": gla_b,
         "gla_norm_w": gla_norm_w, "w_br": w_br, "w_mgate": w_mgate, "w_out": w_out,
         "ffn_w_gu": ffn_w_gu, "ffn_w_down": ffn_w_down}
    cache = {"attn_k": cache_attn_k, "attn_v": cache_attn_v, "mlstm_C": state_mlstm_C,
             "mlstm_n": state_mlstm_n, "mlstm_m": state_mlstm_m, "ret_S": state_ret_S, "gla_S": state_gla_S}
    return _forward(x_prompt, x_sample, cache, c, c_ctx, p, final_norm_w, tm=1024)
```

```python
import functools

import numpy as np
import jax
import jax.numpy as jnp
from jax import lax
from jax.experimental import pallas as pl
from jax.experimental.pallas import tpu as pltpu

F32 = jnp.float32
BF16 = jnp.bfloat16

HEAD_DIM = 128
N_HEADS = 4
A_KV = 2
GRID_W = 64
WINDOW = 128
Q_BLOCK = 128
G_DK = 64
G_RANK = 16
GLA_TAU = 16.0
ROPE_BASE = 10000.0
EPS = 1e-6
MOD_ROWS = 16
SCAN_CHUNK = 128
GLA_CHUNK = 64
SCAN_SEQS = 8
ATT_QBLOCKS = 8
NORM_ROWS = 64
OUT_ROWS = 256
MIX_W = N_HEADS * HEAD_DIM
KV_W = A_KV * HEAD_DIM

Z_AQ = 0
Z_MQ, Z_MK, Z_MV = 512, 1024, 1536
Z_RQ, Z_RK, Z_RV = 2048, 2560, 3072
Z_GQ, Z_GK, Z_GV = 3584, 3840, 4096
ZB_COLS = 4608
Z_AK, Z_AV = 0, 256
Z_MO, Z_RG, Z_GG = 512, 1024, 1536
Z_TAIL = 2048
ZF_COLS = 2304
Z_TN = 1152
TAIL_GLR = 16

VMEM_LIMIT = 56 * 2 ** 20


def _cparams(*sem):
    return pltpu.CompilerParams(dimension_semantics=sem, vmem_limit_bytes=VMEM_LIMIT)


def _log_sigmoid(x):
    return jnp.minimum(x, 0.0) - jnp.log1p(jnp.exp(-jnp.abs(x)))


def _split3(x):
    hi = x.astype(BF16)
    r = x - hi.astype(F32)
    mid = r.astype(BF16)
    lo = (r - mid.astype(F32)).astype(BF16)
    return hi, mid, lo


def _dot(a, b):
    return jnp.dot(a, b, preferred_element_type=F32)


def _dot_nt(a, b):
    return lax.dot_general(a, b, (((1,), (1,)), ((), ())), preferred_element_type=F32)


def _ada_kernel(cond_ref, w_ref, b_ref, o_ref):
    s = jax.nn.silu(cond_ref[...]).astype(BF16)
    o_ref[0] = _dot(s, w_ref[0].astype(BF16)) + b_ref[0]


def ada_modulation(cond, w_ada, b_ada):
    depth, d, n = w_ada.shape
    tn = 1024
    return pl.pallas_call(
        _ada_kernel,
        grid=(depth, n // tn),
        in_specs=[pl.BlockSpec((MOD_ROWS, d), lambda l, j: (0, 0)),
                  pl.BlockSpec((1, d, tn), lambda l, j: (l, 0, j)),
                  pl.BlockSpec((1, 1, tn), lambda l, j: (l, 0, j))],
        out_specs=pl.BlockSpec((1, MOD_ROWS, tn), lambda l, j: (l, 0, j)),
        out_shape=jax.ShapeDtypeStruct((depth, MOD_ROWS, n), F32),
        compiler_params=_cparams("arbitrary", "arbitrary"),
        name="ada_modulation",
    )(cond, w_ada, b_ada.reshape(depth, 1, n))


def _mod_row(grp, tm):
    return grp[0] + (pl.program_id(0) * tm) // grp[1]


def _modulated_norm(x, nw, sh, sc):
    y = x * lax.rsqrt(jnp.mean(x * x, axis=-1, keepdims=True) + EPS)
    return (y * nw) * (1.0 + sc) + sh


def _in_proj_kernel(x_ref, sh_ref, sc_ref, nw_ref, w_ref, zb_ref, zf_ref, h_ref, *, tm, grp, nb):
    j = pl.program_id(1)

    @pl.when(j == 0)
    def _():
        g = _mod_row(grp, tm)
        nw, sh, sc = nw_ref[...], sh_ref[pl.ds(g, 1), :], sc_ref[pl.ds(g, 1), :]
        for r in range(0, tm, NORM_ROWS):
            rows = slice(r, r + NORM_ROWS)
            h_ref[rows, :] = _modulated_norm(x_ref[rows, :], nw, sh, sc).astype(BF16)

    @pl.when(j < nb)
    def _():
        zb_ref[...] = _dot(h_ref[...], w_ref[...]).astype(BF16)

    @pl.when(j >= nb)
    def _():
        zf_ref[...] = _dot(h_ref[...], w_ref[...])


def in_projection(x, sh, sc, nw, w, *, grp, tm):
    rows, d = x.shape
    nb, nf = ZB_COLS // Z_TN, ZF_COLS // Z_TN
    return pl.pallas_call(
        functools.partial(_in_proj_kernel, tm=tm, grp=grp, nb=nb),
        grid=(rows // tm, nb + nf),
        in_specs=[pl.BlockSpec((tm, d), lambda i, j: (i, 0)),
                  pl.BlockSpec((MOD_ROWS, d), lambda i, j: (0, 0)),
                  pl.BlockSpec((MOD_ROWS, d), lambda i, j: (0, 0)),
                  pl.BlockSpec((1, d), lambda i, j: (0, 0)),
                  pl.BlockSpec((d, Z_TN), lambda i, j: (0, j))],
        out_specs=[pl.BlockSpec((tm, Z_TN), lambda i, j: (i, jnp.minimum(j, nb - 1))),
                   pl.BlockSpec((tm, Z_TN), lambda i, j: (i, jnp.maximum(j - nb, 0))),
                   pl.BlockSpec((tm, d), lambda i, j: (i, 0))],
        out_shape=[jax.ShapeDtypeStruct((rows, ZB_COLS), BF16), jax.ShapeDtypeStruct((rows, ZF_COLS), F32),
                   jax.ShapeDtypeStruct((rows, d), BF16)],
        compiler_params=_cparams("arbitrary", "arbitrary"),
        name="in_projection",
    )(x, sh, sc, nw, w)


def _merge_kernel(h_ref, ya_ref, ym_ref, yr_ref, yg_ref, wg_ref, wb_ref, o_ref):
    h = h_ref[...]
    acc = None
    for b, y_ref in enumerate((ya_ref, ym_ref, yr_ref, yg_ref)):
        gate = jax.nn.sigmoid(_dot(h, wg_ref[b]))
        term = gate * _dot(y_ref[...], wb_ref[b])
        acc = term if acc is None else acc + term
    o_ref[...] = acc.astype(BF16)


def merge_branches(h, ys, w_mgate, w_br, *, tm, tn):
    rows, d = h.shape
    nb, mw, _ = w_br.shape
    return pl.pallas_call(
        _merge_kernel,
        grid=(rows // tm, d // tn),
        in_specs=[pl.BlockSpec((tm, d), lambda i, j: (i, 0))]
        + [pl.BlockSpec((tm, mw), lambda i, j: (i, 0))] * nb
        + [pl.BlockSpec((nb, d, tn), lambda i, j: (0, 0, j)),
           pl.BlockSpec((nb, mw, tn), lambda i, j: (0, 0, j))],
        out_specs=pl.BlockSpec((tm, tn), lambda i, j: (i, j)),
        out_shape=jax.ShapeDtypeStruct((rows, d), BF16),
        compiler_params=_cparams("arbitrary", "arbitrary"),
        name="merge_branches",
    )(h, *ys, w_mgate, w_br)


def _residual_proj_kernel(a_ref, w_ref, x_ref, g_ref, o_ref, *, tm, grp):
    g = _mod_row(grp, tm)
    o_ref[...] = x_ref[...] + g_ref[pl.ds(g, 1), :] * _dot(a_ref[...], w_ref[...])


def residual_projection(a, w, x, gate, *, grp, tm, tn):
    rows, k = a.shape
    n = w.shape[1]
    return pl.pallas_call(
        functools.partial(_residual_proj_kernel, tm=tm, grp=grp),
        grid=(rows // tm, n // tn),
        in_specs=[pl.BlockSpec((tm, k), lambda i, j: (i, 0)),
                  pl.BlockSpec((k, tn), lambda i, j: (0, j)),
                  pl.BlockSpec((tm, tn), lambda i, j: (i, j)),
                  pl.BlockSpec((MOD_ROWS, tn), lambda i, j: (0, j))],
        out_specs=pl.BlockSpec((tm, tn), lambda i, j: (i, j)),
        out_shape=jax.ShapeDtypeStruct((rows, n), F32),
        compiler_params=_cparams("arbitrary", "arbitrary"),
        name="residual_projection",
    )(a, w, x, gate)


def _out_proj_norm_kernel(a_ref, w_ref, x_ref, g_ref, sh_ref, sc_ref, nw_ref, o_ref, h_ref, *, tm, grp):
    g = _mod_row(grp, tm)
    gate, nw, sh, sc = g_ref[pl.ds(g, 1), :], nw_ref[...], sh_ref[pl.ds(g, 1), :], sc_ref[pl.ds(g, 1), :]
    for r in range(0, tm, OUT_ROWS):
        rows = slice(r, r + OUT_ROWS)
        x = x_ref[rows, :] + gate * _dot(a_ref[rows, :], w_ref[...])
        o_ref[rows, :] = x
        h_ref[rows, :] = _modulated_norm(x, nw, sh, sc).astype(BF16)


def out_projection_norm(a, w, x, gate, sh, sc, nw, *, grp, tm):
    rows, k = a.shape
    d = w.shape[1]
    full = lambda i: (0, 0)
    return pl.pallas_call(
        functools.partial(_out_proj_norm_kernel, tm=tm, grp=grp),
        grid=(rows // tm,),
        in_specs=[pl.BlockSpec((tm, k), lambda i: (i, 0)),
                  pl.BlockSpec((k, d), full),
                  pl.BlockSpec((tm, d), lambda i: (i, 0)),
                  pl.BlockSpec((MOD_ROWS, d), full),
                  pl.BlockSpec((MOD_ROWS, d), full),
                  pl.BlockSpec((MOD_ROWS, d), full),
                  pl.BlockSpec((1, d), full)],
        out_specs=[pl.BlockSpec((tm, d), lambda i: (i, 0)), pl.BlockSpec((tm, d), lambda i: (i, 0))],
        out_shape=[jax.ShapeDtypeStruct((rows, d), F32), jax.ShapeDtypeStruct((rows, d), BF16)],
        compiler_params=_cparams("arbitrary"),
        name="out_projection_norm",
    )(a, w, x, gate, sh, sc, nw)


def _ffn_up_kernel(h_ref, wg_ref, wu_ref, o_ref):
    h = h_ref[...]
    o_ref[...] = (jax.nn.silu(_dot(h, wg_ref[...])) * _dot(h, wu_ref[...])).astype(BF16)


def ffn_up(h, w_gu, *, tm, tn):
    rows, d = h.shape
    hid = w_gu.shape[1] // 2
    nj = hid // tn
    return pl.pallas_call(
        _ffn_up_kernel,
        grid=(rows // tm, nj),
        in_specs=[pl.BlockSpec((tm, d), lambda i, j: (i, 0)),
                  pl.BlockSpec((d, tn), lambda i, j: (0, j)),
                  pl.BlockSpec((d, tn), lambda i, j: (0, j + nj))],
        out_specs=pl.BlockSpec((tm, tn), lambda i, j: (i, j)),
        out_shape=jax.ShapeDtypeStruct((rows, hid), BF16),
        compiler_params=_cparams("arbitrary", "arbitrary"),
        name="ffn_up",
    )(h, w_gu, w_gu)


def _final_norm_kernel(x_ref, w_ref, o_ref):
    x = x_ref[...]
    o_ref[...] = x * lax.rsqrt(jnp.mean(x * x, axis=-1, keepdims=True) + EPS) * w_ref[...]


def final_norm(x, w, *, tm):
    rows, d = x.shape
    return pl.pallas_call(
        _final_norm_kernel,
        grid=(rows // tm,),
        in_specs=[pl.BlockSpec((tm, d), lambda i: (i, 0)), pl.BlockSpec((1, d), lambda i: (0, 0))],
        out_specs=pl.BlockSpec((tm, d), lambda i: (i, 0)),
        out_shape=jax.ShapeDtypeStruct((rows, d), F32),
        compiler_params=_cparams("arbitrary"),
        name="final_norm",
    )(x, w)


def _sink_softmax(s, sink_col):
    m = jnp.maximum(jnp.max(s, axis=-1, keepdims=True), sink_col)
    p = jnp.exp(s - m)
    return p * (1.0 / (jnp.sum(p, axis=-1, keepdims=True) + jnp.exp(sink_col - m)))


def _sink_column(sink_ref, kv, rows):
    half = lax.broadcasted_iota(jnp.int32, (2 * rows, 1), 0) < rows
    return jnp.where(half, sink_ref[:, 2 * kv:2 * kv + 1], sink_ref[:, 2 * kv + 1:2 * kv + 2])


def _ctx_attn_kernel(q_ref, k_ref, v_ref, sink_ref, o_ref):
    t = q_ref.shape[0]
    scale = HEAD_DIM ** -0.5
    scores = {}
    for kv in range(A_KV):
        lo = 2 * kv * HEAD_DIM
        q2 = jnp.concatenate([q_ref[:, lo:lo + HEAD_DIM], q_ref[:, lo + HEAD_DIM:lo + 2 * HEAD_DIM]], axis=0)
        k = k_ref[:, kv * HEAD_DIM:(kv + 1) * HEAD_DIM].astype(BF16)
        scores[kv] = _dot_nt(q2.astype(BF16), k) * scale
    for kv in range(A_KV):
        lo = 2 * kv * HEAD_DIM
        v = v_ref[:, kv * HEAD_DIM:(kv + 1) * HEAD_DIM].astype(BF16)
        p = _sink_softmax(scores[kv], _sink_column(sink_ref, kv, t))
        o = _dot(p.astype(BF16), v)
        o_ref[:, lo:lo + HEAD_DIM] = o[:t].astype(BF16)
        o_ref[:, lo + HEAD_DIM:lo + 2 * HEAD_DIM] = o[t:].astype(BF16)


def context_attention(zb, zf, sink_row):
    batch, seq, _ = zb.shape
    return pl.pallas_call(
        _ctx_attn_kernel,
        grid=(batch,),
        in_specs=[pl.BlockSpec((None, seq, MIX_W), lambda b: (b, 0, Z_AQ // MIX_W)),
                  pl.BlockSpec((None, seq, KV_W), lambda b: (b, 0, Z_AK // KV_W)),
                  pl.BlockSpec((None, seq, KV_W), lambda b: (b, 0, Z_AV // KV_W)),
                  pl.BlockSpec((1, 128), lambda b: (0, 0))],
        out_specs=pl.BlockSpec((None, seq, MIX_W), lambda b: (b, 0, 0)),
        out_shape=jax.ShapeDtypeStruct((batch, seq, MIX_W), BF16),
        compiler_params=_cparams("arbitrary"),
        name="context_attention",
    )(zb, zf, zf, sink_row)


def _rope(x, cos, sin_signed):
    lane = lax.broadcasted_iota(jnp.int32, x.shape, 1)
    swapped = jnp.where(lane % 64 < 32, pltpu.roll(x, 96, 1), pltpu.roll(x, 32, 1))
    return x * cos + swapped * sin_signed


def _rope_kernel(q_ref, k_ref, cos_ref, sin_ref, qo_ref, ko_ref):
    cos, sin = cos_ref[...], sin_ref[...]
    for h in range(N_HEADS):
        sl = slice(h * HEAD_DIM, (h + 1) * HEAD_DIM)
        qo_ref[:, sl] = _rope(q_ref[:, sl].astype(F32), cos, sin).astype(BF16)
    for h in range(A_KV):
        sl = slice(h * HEAD_DIM, (h + 1) * HEAD_DIM)
        ko_ref[:, sl] = _rope(k_ref[:, sl], cos, sin).astype(BF16)


def rope_tables(seq):
    pos = np.arange(seq)
    quarter = HEAD_DIM // 4
    inv = jnp.asarray(ROPE_BASE, F32) ** (-jnp.arange(quarter, dtype=F32) / quarter)
    row = jnp.asarray(pos // GRID_W, F32)[:, None] * inv[None, :]
    col = jnp.asarray(pos % GRID_W, F32)[:, None] * inv[None, :]
    cos = jnp.concatenate([jnp.cos(row), jnp.cos(row), jnp.cos(col), jnp.cos(col)], axis=1)
    sin = jnp.concatenate([-jnp.sin(row), jnp.sin(row), -jnp.sin(col), jnp.sin(col)], axis=1)
    return cos, sin


def rope_qk(zb, zf, cos, sin, *, tb):
    batch, seq, _ = zb.shape
    return pl.pallas_call(
        _rope_kernel,
        grid=(batch, seq // tb),
        in_specs=[pl.BlockSpec((None, tb, MIX_W), lambda b, t: (b, t, Z_AQ // MIX_W)),
                  pl.BlockSpec((None, tb, KV_W), lambda b, t: (b, t, Z_AK // KV_W)),
                  pl.BlockSpec((tb, HEAD_DIM), lambda b, t: (t, 0)),
                  pl.BlockSpec((tb, HEAD_DIM), lambda b, t: (t, 0))],
        out_specs=[pl.BlockSpec((None, tb, MIX_W), lambda b, t: (b, t, 0)),
                   pl.BlockSpec((None, tb, KV_W), lambda b, t: (b, t, 0))],
        out_shape=[jax.ShapeDtypeStruct((batch, seq, MIX_W), BF16),
                   jax.ShapeDtypeStruct((batch, seq, KV_W), BF16)],
        compiler_params=_cparams("arbitrary", "arbitrary"),
        name="rope_qk",
    )(zb, zf, cos, sin)


def _lat_attn_kernel(q_ref, kp_ref, kc_ref, kn_ref, vp_ref, vc_ref, vn_ref, kx_ref, vx_ref, sink_ref,
                     o_ref, *, seq):
    nq = Q_BLOCK
    past = kx_ref.shape[0]
    scale = HEAD_DIM ** -0.5
    ncol = 3 * nq + past
    jj = lax.broadcasted_iota(jnp.int32, (2 * nq, ncol), 1)
    tt = lax.broadcasted_iota(jnp.int32, (2 * nq, ncol), 0) % nq
    in_band = jnp.abs(tt + nq - jj) <= WINDOW
    units = [(j, kv) for j in range(ATT_QBLOCKS) for kv in range(A_KV)]
    scores, ok = {}, {}
    for j in range(ATT_QBLOCKS):
        kpos = (pl.program_id(1) * ATT_QBLOCKS + j) * nq - nq + jj
        ok[j] = (jj >= 3 * nq) | (in_band & (kpos >= 0) & (kpos < seq))
    for j, kv in units:
        lo = 2 * kv * HEAD_DIM
        ks = slice(kv * HEAD_DIM, (kv + 1) * HEAD_DIM)
        rows = slice(j * nq, (j + 1) * nq)
        kblocks = [kp_ref[:, ks]] + [kc_ref[i * nq:(i + 1) * nq, ks] for i in range(ATT_QBLOCKS)] + [kn_ref[:, ks]]
        q2 = jnp.concatenate([q_ref[rows, lo:lo + HEAD_DIM], q_ref[rows, lo + HEAD_DIM:lo + 2 * HEAD_DIM]], axis=0)
        kcat = jnp.concatenate(kblocks[j:j + 3] + [kx_ref[:, ks].astype(BF16)], axis=0)
        scores[j, kv] = _dot_nt(q2, kcat)
    probs = {}
    for j, kv in units:
        s = jnp.where(ok[j], scores[j, kv] * scale, -jnp.inf)
        probs[j, kv] = _sink_softmax(s, _sink_column(sink_ref, kv, nq)).astype(BF16)
    for j, kv in units:
        lo = 2 * kv * HEAD_DIM
        ks = slice(kv * HEAD_DIM, (kv + 1) * HEAD_DIM)
        rows = slice(j * nq, (j + 1) * nq)
        vblocks = [vp_ref[:, ks]] + [vc_ref[i * nq:(i + 1) * nq, ks] for i in range(ATT_QBLOCKS)] + [vn_ref[:, ks]]
        vcat = jnp.concatenate([blk.astype(BF16) for blk in vblocks[j:j + 3]] + [vx_ref[:, ks].astype(BF16)], axis=0)
        o = _dot(probs[j, kv], vcat)
        o_ref[rows, lo:lo + HEAD_DIM] = o[:nq].astype(BF16)
        o_ref[rows, lo + HEAD_DIM:lo + 2 * HEAD_DIM] = o[nq:].astype(BF16)


def latent_attention(qr, kr, zf, cache_k, cache_v, sink_row, *, layer):
    batch, seq, _ = zf.shape
    nb = seq // Q_BLOCK
    past = cache_k.shape[2]
    wide = ATT_QBLOCKS * Q_BLOCK

    def prev(q):
        return jnp.maximum(q * ATT_QBLOCKS - 1, 0)

    def nxt(q):
        return jnp.minimum((q + 1) * ATT_QBLOCKS, nb - 1)

    vcol = Z_AV // KV_W
    return pl.pallas_call(
        functools.partial(_lat_attn_kernel, seq=seq),
        grid=(batch, nb // ATT_QBLOCKS),
        in_specs=[pl.BlockSpec((None, wide, MIX_W), lambda b, q: (b, q, 0)),
                  pl.BlockSpec((None, Q_BLOCK, KV_W), lambda b, q: (b, prev(q), 0)),
                  pl.BlockSpec((None, wide, KV_W), lambda b, q: (b, q, 0)),
                  pl.BlockSpec((None, Q_BLOCK, KV_W), lambda b, q: (b, nxt(q), 0)),
                  pl.BlockSpec((None, Q_BLOCK, KV_W), lambda b, q: (b, prev(q), vcol)),
                  pl.BlockSpec((None, wide, KV_W), lambda b, q: (b, q, vcol)),
                  pl.BlockSpec((None, Q_BLOCK, KV_W), lambda b, q: (b, nxt(q), vcol)),
                  pl.BlockSpec((None, None, past, KV_W), lambda b, q: (b, layer, 0, 0)),
                  pl.BlockSpec((None, None, past, KV_W), lambda b, q: (b, layer, 0, 0)),
                  pl.BlockSpec((1, 128), lambda b, q: (0, 0))],
        out_specs=pl.BlockSpec((None, wide, MIX_W), lambda b, q: (b, q, 0)),
        out_shape=jax.ShapeDtypeStruct((batch, seq, MIX_W), BF16),
        compiler_params=_cparams("arbitrary", "arbitrary"),
        name="latent_attention",
    )(qr, kr, kr, kr, zf, zf, zf, cache_k, cache_v, sink_row)


def _store_gated_head_norm(total, o_ref, g_ref, nw_ref, act):
    inv = {u: lax.rsqrt(jnp.mean(t * t, axis=-1, keepdims=True) + EPS) for u, t in total.items()}
    for (sb, h), t in total.items():
        sl = slice(h * HEAD_DIM, (h + 1) * HEAD_DIM)
        o_ref[sb, :, sl] = (act(g_ref[sb, :, sl]) * ((t * inv[sb, h]) * nw_ref[:, sl])).astype(BF16)


def _tri_mask(n, rev):
    t = lax.broadcasted_iota(jnp.int32, (n, n), 0)
    s = lax.broadcasted_iota(jnp.int32, (n, n), 1)
    return (s >= t) if rev else (s <= t)


def _chunk_index(rev, n):
    return (lambda c: n - 1 - c) if rev else (lambda c: c)


def _scan_specs(L, width, cidx):
    def zspec(col):
        return pl.BlockSpec((SCAN_SEQS, L, width), lambda b, c: (b, cidx(c), col // width))
    return zspec


def _state_spec(*dims):
    return pl.BlockSpec((SCAN_SEQS,) + dims, lambda b, c: (b,) + (0,) * len(dims))


def _mlstm_kernel(*refs, rev, d, combine):
    if combine:
        (q_ref, k_ref, v_ref, t_ref, bias_ref, c0_ref, n0_ref, m0_ref, hp_ref, g_ref, nw_ref,
         o_ref, c_ref, n_ref, m_ref, cn_ref) = refs
    else:
        (q_ref, k_ref, v_ref, t_ref, bias_ref, c0_ref, n0_ref, m0_ref,
         o_ref, c_ref, n_ref, m_ref, cn_ref) = refs
    L = SCAN_CHUNK
    DK = HEAD_DIM

    @pl.when(pl.program_id(1) == 0)
    def _():
        m_ref[...] = m0_ref[...]
        for sb in range(SCAN_SEQS):
            for h in range(N_HEADS):
                cn_ref[sb, h, :, :DK] = c0_ref[sb, h]
                cn_ref[sb, h, :, DK:] = jnp.broadcast_to(n0_ref[sb, h:h + 1, :], (DK, DK)).T

    tri = _tri_mask(L, rev)
    tri_b = tri.astype(BF16)
    last = 0 if rev else L - 1
    scale = HEAD_DIM ** -0.5
    gates, cum = {}, {}
    for sb in range(SCAN_SEQS):
        gates[sb] = t_ref[sb] + bias_ref[...]
        hi, mid, lo = _split3(_log_sigmoid(gates[sb]))
        cum[sb] = _dot(tri_b, hi) + _dot(tri_b, mid) + _dot(tri_b, lo)
    units = [(sb, h) for sb in range(SCAN_SEQS) for h in range(N_HEADS)]
    gates_t = {sb: gates[sb].T for sb in range(SCAN_SEQS)}
    cum_t = {sb: cum[sb].T for sb in range(SCAN_SEQS)}
    m_all = {sb: m_ref[sb] for sb in range(SCAN_SEQS)}
    ones = jnp.ones((L, DK), BF16)
    col, qb, ks, v1 = {}, {}, {}, {}
    for u in units:
        sb, h = u
        ci, cf = d * 8 + h, d * 8 + 4 + h
        sl = slice(h * HEAD_DIM, (h + 1) * HEAD_DIM)
        i_col, b_col = gates[sb][:, ci:ci + 1], cum[sb][:, cf:cf + 1]
        m = m_all[sb][:, h:h + 1]
        bl = b_col[last:last + 1, :]
        g = bl - b_col + i_col
        m_new = jnp.maximum(bl + m, jnp.max(g, axis=0, keepdims=True))
        col[u] = dict(m=m, b=b_col, m_new=m_new, w_old=jnp.exp(bl + m - m_new), e=jnp.exp(g - m_new))
        qb[u] = q_ref[sb, :, sl].astype(BF16)
        ks[u] = k_ref[sb, :, sl].astype(F32) * scale
        v1[u] = jnp.concatenate([v_ref[sb, :, sl].astype(BF16), ones], axis=1)
    kw_t = {u: (col[u]["e"] * ks[u]).T.astype(BF16) for u in units}
    s_raw, q_cn = {}, {}
    for u in units:
        sb, h = u
        cn_old = cn_ref[sb, h]
        s_raw[u] = _dot_nt(qb[u], ks[u].astype(BF16))
        q_cn[u] = _dot(qb[u], cn_old.astype(BF16))
        cn_ref[sb, h] = col[u]["w_old"] * cn_old + _dot(kw_t[u], v1[u])
    wmat = {}
    for u in units:
        sb, h = u
        ci, cf = d * 8 + h, d * 8 + 4 + h
        r_row = gates_t[sb][ci:ci + 1, :] - cum_t[sb][cf:cf + 1, :]
        wmat[u] = jnp.where(tri, r_row, -jnp.inf)
    cmax = {u: jnp.max(wmat[u], axis=1, keepdims=True) for u in units}
    u_bc, s = {}, {}
    for u in units:
        u_bc[u] = jnp.maximum(jnp.broadcast_to(col[u]["m"], (L, L)), jnp.broadcast_to(cmax[u], (L, L)))
        s[u] = (s_raw[u] * jnp.exp(wmat[u] - u_bc[u])).astype(BF16)
    sv1 = {u: _dot(s[u], v1[u]) for u in units}
    total = {}
    for u in units:
        sb, h = u
        sl = slice(h * HEAD_DIM, (h + 1) * HEAD_DIM)
        w_state = jnp.exp(col[u]["m"] - u_bc[u])
        den = w_state * q_cn[u][:, DK:] + sv1[u][:, DK:]
        floor = jnp.exp(-(jnp.broadcast_to(col[u]["b"], (L, L)) + u_bc[u]))
        hh = (w_state * q_cn[u][:, :DK] + sv1[u][:, :DK]) * (1.0 / jnp.maximum(jnp.abs(den), floor))
        if combine:
            total[u] = hh + hp_ref[sb, :, sl]
        else:
            o_ref[sb, :, sl] = hh
    if combine:
        _store_gated_head_norm(total, o_ref, g_ref, nw_ref, jax.nn.sigmoid)
    for u in units:
        sb, h = u
        m_ref[sb, :, h:h + 1] = col[u]["m_new"]

    @pl.when(pl.program_id(1) == pl.num_programs(1) - 1)
    def _():
        for sb in range(SCAN_SEQS):
            for h in range(N_HEADS):
                c_ref[sb, h] = cn_ref[sb, h, :, :DK]
                n_ref[sb, h:h + 1, :] = cn_ref[sb, h, :, DK:].T[0:1, :]


def mlstm_direction(zb, zf, bias_row, c0, n0, m0, *, d, prev=None, norm_w=None):
    batch, seq, _ = zb.shape
    L =SCAN_CHUNK
    nc = seq // L
    rev = d == 1
    cidx = _chunk_index(rev, nc)
    combine = prev is not None
    zspec = _scan_specs(L, MIX_W, cidx)
    state_specs = [_state_spec(N_HEADS, HEAD_DIM, HEAD_DIM), _state_spec(N_HEADS, HEAD_DIM),
                   _state_spec(1, N_HEADS)]
    in_specs = [zspec(Z_MQ), zspec(Z_MK), zspec(Z_MV),
                pl.BlockSpec((SCAN_SEQS, L, 128), lambda b, c: (b, cidx(c), Z_TAIL // 128)),
                pl.BlockSpec((1, 128), lambda b, c: (0, 0))] + state_specs
    args = [zb, zb, zb, zf, bias_row, c0, n0, m0]
    if combine:
        in_specs += [zspec(0), zspec(Z_MO), pl.BlockSpec((1, MIX_W), lambda b, c: (0, 0))]
        args += [prev, zf, norm_w]
    return pl.pallas_call(
        functools.partial(_mlstm_kernel, rev=rev, d=d, combine=combine),
        grid=(batch // SCAN_SEQS, nc),
        in_specs=in_specs,
        out_specs=[zspec(0)] + state_specs,
        out_shape=[jax.ShapeDtypeStruct((batch, seq, MIX_W), BF16 if combine else F32),
                   jax.ShapeDtypeStruct((batch, N_HEADS, HEAD_DIM, HEAD_DIM), F32),
                   jax.ShapeDtypeStruct((batch, N_HEADS, HEAD_DIM), F32),
                   jax.ShapeDtypeStruct((batch, 1, N_HEADS), F32)],
        scratch_shapes=[pltpu.VMEM((SCAN_SEQS, N_HEADS, HEAD_DIM, 2 * HEAD_DIM), F32)],
        compiler_params=_cparams("arbitrary", "arbitrary"),
        name="mlstm_scan",
    )(*args)


def _retention_kernel(*refs, rev, combine):
    if combine:
        q_ref, k_ref, v_ref, dec_ref, s0_ref, hp_ref, g_ref, nw_ref, o_ref, s_ref = refs
    else:
        q_ref, k_ref, v_ref, dec_ref, s0_ref, o_ref, s_ref = refs
    L = SCAN_CHUNK

    @pl.when(pl.program_id(1) == 0)
    def _():
        s_ref[...] = s0_ref[...]

    tri = _tri_mask(L, rev)
    t_i = lax.broadcasted_iota(jnp.int32, (L, L), 0)
    s_i = lax.broadcasted_iota(jnp.int32, (L, L), 1)
    dist = jnp.abs(t_i - s_i).astype(F32)
    pos = lax.broadcasted_iota(jnp.int32, (L, 1), 0).astype(F32)
    xi_pow = (L - pos) if rev else (pos + 1.0)
    zeta_pow = pos if rev else (L - 1.0 - pos)
    log_gamma = _log_sigmoid(dec_ref[...])
    scale = HEAD_DIM ** -0.5
    units = [(sb, h) for h in range(N_HEADS) for sb in range(SCAN_SEQS)]
    stage1 = {}
    for sb, h in units:
        sl = slice(h * HEAD_DIM, (h + 1) * HEAD_DIM)
        lg = log_gamma[:, h:h + 1]
        qb = q_ref[sb, :, sl].astype(BF16)
        ks = k_ref[sb, :, sl].astype(F32) * scale
        vb = v_ref[sb, :, sl].astype(BF16)
        s_old = s_ref[sb, h]
        att = _dot_nt(qb, ks.astype(BF16))
        inter = _dot(qb, s_old.astype(BF16))
        outer = _dot((ks * jnp.exp(zeta_pow * lg)).T.astype(BF16), vb)
        s_ref[sb, h] = jnp.exp(L * lg) * s_old + outer
        stage1[sb, h] = (att, inter, vb)
    decay, xi = {}, {}
    for h in range(N_HEADS):
        lg = log_gamma[:, h:h + 1]
        decay[h] = jnp.where(tri, jnp.exp(dist * lg), 0.0)
        xi[h] = jnp.exp(xi_pow * lg)
    total = {}
    for sb, h in units:
        sl = slice(h * HEAD_DIM, (h + 1) * HEAD_DIM)
        att, inter, vb = stage1[sb, h]
        y = _dot((att * decay[h]).astype(BF16), vb) + xi[h] * inter
        if combine:
            total[sb, h] = y + hp_ref[sb, :, sl]
        else:
            o_ref[sb, :, sl] = y
    if combine:
        _store_gated_head_norm(total, o_ref, g_ref, nw_ref, jax.nn.silu)


def retention_direction(zb, zf, decay_row, s0, *, d, prev=None, norm_w=None):
    batch, seq, _ = zb.shape
    L =SCAN_CHUNK
    nc = seq // L
    rev = d == 1
    cidx = _chunk_index(rev, nc)
    combine = prev is not None
    zspec = _scan_specs(L, MIX_W, cidx)
    state_spec = _state_spec(N_HEADS, HEAD_DIM, HEAD_DIM)
    in_specs = [zspec(Z_RQ), zspec(Z_RK), zspec(Z_RV), pl.BlockSpec((1, 128), lambda b, c: (0, 0)), state_spec]
    args = [zb, zb, zb, decay_row, s0]
    if combine:
        in_specs += [zspec(0), zspec(Z_RG), pl.BlockSpec((1, MIX_W), lambda b, c: (0, 0))]
        args += [prev, zf, norm_w]
    return pl.pallas_call(
        functools.partial(_retention_kernel, rev=rev, combine=combine),
        grid=(batch // SCAN_SEQS, nc),
        in_specs=in_specs,
        out_specs=[zspec(0), state_spec],
        out_shape=[jax.ShapeDtypeStruct((batch, seq, MIX_W), BF16 if combine else F32),
                   jax.ShapeDtypeStruct((batch, N_HEADS, HEAD_DIM, HEAD_DIM), F32)],
        compiler_params=_cparams("arbitrary", "arbitrary"),
        name="retention_scan",
    )(*args)


GLA_LEVELS = (32, 16, 8, 4, 2, 1)
GLA_LANES = N_HEADS * G_DK


def _gla_constants(rev):
    L = GLA_CHUNK
    t = np.arange(L)[:, None]
    r = np.arange(L)[None, :]
    mats = [(r <= t).astype(np.float32)]
    pair = [np.eye(L, dtype=np.float32)]
    for w in GLA_LEVELS:
        blk_t, blk_r = t // (2 * w), r // (2 * w)
        ref_t = blk_t * 2 * w + w - 1
        upper_t = (t % (2 * w)) >= w
        mats.append(((upper_t & (r > ref_t) & (r <= t)) | ((~upper_t) & (r > t) & (r <= ref_t))).astype(np.float32))
        lower_r = (r % (2 * w)) < w
        pair.append((upper_t & lower_r & (blk_t == blk_r)).astype(np.float32))
    if rev:
        mats = [m[::-1, ::-1] for m in mats]
        pair = [p[::-1, ::-1] for p in pair]
    big = np.concatenate(mats, axis=0)
    big3 = np.concatenate([big, big, big], axis=1)
    pair = np.stack([np.tile(p, (1, 2)) for p in pair])
    return jnp.asarray(big3, BF16), jnp.asarray(pair, F32)


def _gla_kernel(*refs, rev, combine):
    if combine:
        (q_ref, k_ref, v_ref, t_ref, w2_ref, gb_ref, big_ref, pair_ref, s0_ref,
         hp_ref, g_ref, nw_ref, o_ref, so_ref, s_ref) = refs
    else:
        (q_ref, k_ref, v_ref, t_ref, w2_ref, gb_ref, big_ref, pair_ref, s0_ref,
         o_ref, so_ref, s_ref) = refs
    L = GLA_CHUNK
    c = pl.program_id(1)
    pairs = N_HEADS // 2
    pw = 2 * G_DK
    ow = 2 * HEAD_DIM

    @pl.when(c == 0)
    def _():
        s_ref[...] = jnp.zeros_like(s_ref)
        for sb in range(SCAN_SEQS):
            for h in range(N_HEADS):
                hl = h % 2
                s_ref[sb, h // 2, hl * G_DK:(hl + 1) * G_DK, hl * HEAD_DIM:(hl + 1) * HEAD_DIM] = s0_ref[sb, h]

    last = 0 if rev else L - 1
    first_head = lax.broadcasted_iota(jnp.int32, (L, pw), 1) < G_DK
    diag_block = ((lax.broadcasted_iota(jnp.int32, (pw, ow), 0) < G_DK)
                  == (lax.broadcasted_iota(jnp.int32, (pw, ow), 1) < HEAD_DIM))
    zeros_v = jnp.zeros((L, HEAD_DIM), BF16)

    def stacked(x):
        return jnp.concatenate([jnp.where(first_head, x, 0.0), jnp.where(first_head, 0.0, x)], axis=0).astype(BF16)

    seqs = range(SCAN_SEQS)
    units = [(sb, p) for sb in seqs for p in range(pairs)]
    logit = {sb: _dot(t_ref[sb].astype(BF16), w2_ref[...]) + gb_ref[...] for sb in seqs}
    e = {}
    for sb in seqs:
        la = _log_sigmoid(logit[sb]) * (1.0 / GLA_TAU)
        e[sb] = _dot(big_ref[...], jnp.concatenate(_split3(la), axis=0))
    pad = jnp.zeros((128 - L, pw), F32)
    pad_v = jnp.zeros((128 - L, ow), BF16)
    stage = {}
    for sb, p in units:
        lanes = slice(p * pw, (p + 1) * pw)
        cum = e[sb][0:L, lanes]
        q = q_ref[sb, :, lanes].astype(F32) * (G_DK ** -0.5)
        k = k_ref[sb, :, lanes].astype(F32)
        vb = v_ref[sb, :, p * ow:(p + 1) * ow].astype(BF16)
        s_old = s_ref[sb, p]
        att = pair_ref[0] * _dot_nt(q.astype(BF16), stacked(k))
        for i in range(len(GLA_LEVELS)):
            x = jnp.exp(e[sb][(i + 1) * L:(i + 2) * L, lanes])
            att = att + pair_ref[i + 1] * _dot_nt((q * x).astype(BF16), stacked(k * x))
        inter = _dot((q * jnp.exp(cum)).astype(BF16), s_old.astype(BF16))
        last_row = cum[last:last + 1, :]
        kd_t = jnp.concatenate([k * jnp.exp(last_row - cum), pad], axis=0).T
        cum_t = jnp.concatenate([cum, pad], axis=0).T
        upd = _dot(kd_t.astype(BF16), jnp.concatenate([vb, pad_v], axis=0))
        s_ref[sb, p] = jnp.exp(cum_t[:, last:last + 1]) * s_old + jnp.where(diag_block, upd, 0.0)
        stage[sb, p] = (att.astype(BF16), inter, vb)
    total = {}
    for sb, p in units:
        att, inter, vb = stage[sb, p]
        vblk = jnp.concatenate([jnp.concatenate([vb[:, :HEAD_DIM], zeros_v], axis=1),
                                jnp.concatenate([zeros_v, vb[:, HEAD_DIM:]], axis=1)], axis=0)
        y = _dot(att, vblk) + inter
        for hl in range(2):
            h = 2 * p + hl
            sl = slice(h * HEAD_DIM, (h + 1) * HEAD_DIM)
            yh = y[:, hl * HEAD_DIM:(hl + 1) * HEAD_DIM]
            if combine:
                total[sb, h] = yh + hp_ref[sb, :, sl]
            else:
                o_ref[sb, :, sl] = yh
    if combine:
        _store_gated_head_norm(total, o_ref, g_ref, nw_ref, jax.nn.silu)

    @pl.when(c == pl.num_programs(1) - 1)
    def _():
        for sb in range(SCAN_SEQS):
            for h in range(N_HEADS):
                hl = h % 2
                so_ref[sb, h] = s_ref[sb, h // 2, hl * G_DK:(hl + 1) * G_DK, hl * HEAD_DIM:(hl + 1) * HEAD_DIM]


def gla_direction(zb, zf, w2pad, gbias, s0, *, d, prev=None, norm_w=None):
    batch, seq, _ = zb.shape
    L =GLA_CHUNK
    nc = seq // L
    rev = d == 1
    cidx = _chunk_index(rev, nc)
    combine = prev is not None
    big3, pair = _gla_constants(rev)

    def const(a):
        return pl.BlockSpec(a.shape, lambda b, c: (0,) * a.ndim)

    zq = _scan_specs(L, GLA_LANES, cidx)
    zv = _scan_specs(L, MIX_W, cidx)
    state_spec = _state_spec(N_HEADS, G_DK, HEAD_DIM)
    in_specs = [zq(Z_GQ), zq(Z_GK), zv(Z_GV),
                pl.BlockSpec((SCAN_SEQS, L, 128), lambda b, c: (b, cidx(c), Z_TAIL // 128)),
                const(w2pad), const(gbias), const(big3), const(pair), state_spec]
    args = [zb, zb, zb, zf, w2pad, gbias, big3, pair, s0]
    if combine:
        in_specs += [zv(0), zv(Z_GG), pl.BlockSpec((1, MIX_W), lambda b, c: (0, 0))]
        args += [prev, zf, norm_w]
    return pl.pallas_call(
        functools.partial(_gla_kernel, rev=rev, combine=combine),
        grid=(batch // SCAN_SEQS, nc),
        in_specs=in_specs,
        out_specs=[zv(0), state_spec],
        out_shape=[jax.ShapeDtypeStruct((batch, seq, MIX_W), BF16 if combine else F32),
                   jax.ShapeDtypeStruct((batch, N_HEADS, G_DK, HEAD_DIM), F32)],
        scratch_shapes=[pltpu.VMEM((SCAN_SEQS, N_HEADS // 2, 2 * G_DK, 2 * HEAD_DIM), F32)],
        compiler_params=_cparams("arbitrary", "arbitrary"),
        name="gla_scan",
    )(*args)


def _pad_row(v, width=128):
    v = v.reshape(1, -1).astype(F32)
    return jnp.pad(v, ((0, 0), (0, width - v.shape[1])))


def _permute_w_in(w_in_l):
    sizes = (512, 256, 256, 512, 512, 512, 512, 16, 512, 512, 512, 512, 256, 256, 512, 512, 32)
    offs = np.concatenate([[0], np.cumsum(sizes)])
    pieces = [w_in_l[:, offs[i]:offs[i + 1]].astype(BF16) for i in range(len(sizes))]
    order = [0, 3, 4, 5, 8, 9, 10, 12, 13, 14,
             1, 2, 6, 11, 15, 7, 16]
    w = jnp.concatenate([pieces[i] for i in order], axis=1)
    return jnp.pad(w, ((0, 0), (0, ZB_COLS + ZF_COLS - w.shape[1])))


def _prepare_layer(p, l):
    d = p["w_in"].shape[1]
    w2pad = []
    for direction in range(2):
        lo = TAIL_GLR + direction * G_RANK
        w2pad.append(jnp.zeros((128, GLA_LANES), F32).at[lo:lo + G_RANK].set(p["gla_w2"][l, direction]).astype(BF16))
    return {
        "w_in": _permute_w_in(p["w_in"][l]),
        "w_mgate": p["w_mgate"][l].astype(BF16), "w_br": p["w_br"][l].astype(BF16),
        "w_out": p["w_out"][l].astype(BF16), "ffn_w_gu": p["ffn_w_gu"][l].astype(BF16),
        "ffn_w_down": p["ffn_w_down"][l].astype(BF16),
        "norm1_w": p["norm1_w"][l].reshape(1, d), "norm2_w": p["norm2_w"][l].reshape(1, d),
        "sink_row": _pad_row(p["attn_sink"][l]), "mlstm_bias": _pad_row(p["mlstm_if_b"][l]),
        "mlstm_norm_w": p["mlstm_norm_w"][l].reshape(1, MIX_W),
        "ret_decay": [_pad_row(p["ret_decay"][l, 0]), _pad_row(p["ret_decay"][l, 1])],
        "ret_norm_w": p["ret_norm_w"][l].reshape(1, MIX_W),
        "gla_w2": w2pad, "gla_b": [p["gla_b"][l, 0].reshape(1, GLA_LANES), p["gla_b"][l, 1].reshape(1, GLA_LANES)],
        "gla_norm_w": p["gla_norm_w"][l].reshape(1, MIX_W),
    }


def _mixers(zb, zf, w, l, cache):
    batch, seq, _ = zb.shape
    is_ctx = cache is None
    if is_ctx:
        ya = context_attention(zb, zf, w["sink_row"])
    else:
        cos, sin = rope_tables(seq)
        qr, kr = rope_qk(zb, zf, cos, sin, tb=min(seq, 512))
        ck = cache["attn_k"].reshape(cache["attn_k"].shape[:3] + (KV_W,))
        cv = cache["attn_v"].reshape(cache["attn_v"].shape[:3] + (KV_W,))
        ya = latent_attention(qr, kr, zf, ck, cv, w["sink_row"], layer=l)

    def zeros(*dims):
        return jnp.zeros((batch,) + dims, F32)

    res = {}
    for d in (1, 0):
        if is_ctx:
            init = (zeros(N_HEADS, HEAD_DIM, HEAD_DIM), zeros(N_HEADS, HEAD_DIM), zeros(1, N_HEADS))
        else:
            init = (cache["mlstm_C"][:, l, d], cache["mlstm_n"][:, l, d],
                    cache["mlstm_m"][:, l, d].reshape(batch, 1, N_HEADS))
        res[d] = mlstm_direction(zb, zf, w["mlstm_bias"], *init, d=d, prev=res[1][0] if d == 0 else None,
                                 norm_w=w["mlstm_norm_w"] if d == 0 else None)
    ym = res[0][0]
    states = {"mlstm_C": jnp.stack([res[0][1], res[1][1]], axis=1),
              "mlstm_n": jnp.stack([res[0][2], res[1][2]], axis=1),
              "mlstm_m": jnp.stack([res[0][3][:, 0], res[1][3][:, 0]], axis=1)}

    res = {}
    for d in (1, 0):
        s0 = zeros(N_HEADS, HEAD_DIM, HEAD_DIM) if is_ctx else cache["ret_S"][:, l, d]
        res[d] = retention_direction(zb, zf, w["ret_decay"][d], s0, d=d, prev=res[1][0] if d == 0 else None,
                                     norm_w=w["ret_norm_w"] if d == 0 else None)
    yr = res[0][0]
    states["ret_S"] = jnp.stack([res[0][1], res[1][1]], axis=1)

    res = {}
    for d in (1, 0):
        s0 = zeros(N_HEADS, G_DK, HEAD_DIM) if is_ctx else cache["gla_S"][:, l, d]
        res[d] = gla_direction(zb, zf, w["gla_w2"][d], w["gla_b"][d], s0, d=d, prev=res[1][0] if d == 0 else None,
                               norm_w=w["gla_norm_w"] if d == 0 else None)
    yg = res[0][0]
    states["gla_S"] = jnp.stack([res[0][1], res[1][1]], axis=1)

    rows = batch * seq
    ys = [y.reshape(rows, MIX_W) for y in (ya, ym, yr, yg)]
    states["attn_k"] = zf[:, :, Z_AK:Z_AK + KV_W].reshape(batch, seq, A_KV, HEAD_DIM)
    states["attn_v"] = zf[:, :, Z_AV:Z_AV + KV_W].reshape(batch, seq, A_KV, HEAD_DIM)
    return ys, states


def _layer(x, dims, mod_l, w, l, cache, *, grp, tm):
    batch, seq = dims
    d = x.shape[1]
    sh1, sc1, g1, sh2, sc2, g2 = (mod_l[:, i * d:(i + 1) * d] for i in range(6))
    zb, zf, h = in_projection(x, sh1, sc1, w["norm1_w"], w["w_in"], grp=grp, tm=tm)
    ys, states = _mixers(zb.reshape(batch, seq, ZB_COLS), zf.reshape(batch, seq, ZF_COLS), w, l, cache)
    merged = merge_branches(h, ys, w["w_mgate"], w["w_br"], tm=tm, tn=256)
    x, h2 = out_projection_norm(merged, w["w_out"], x, g1, sh2, sc2, w["norm2_w"], grp=grp, tm=tm // 2)
    act = ffn_up(h2, w["ffn_w_gu"], tm=tm, tn=512)
    x = residual_projection(act, w["ffn_w_down"], x, g2, grp=grp, tm=tm, tn=512)
    return x, states


def _forward(x_prompt, x_sample, cache, c, c_ctx, p, final_norm_w, *, tm):
    cb, ct, d = x_prompt.shape
    lb, lt, _ = x_sample.shape
    assert (cb * ct) % tm == 0 and lt % tm == 0 and lb + 1 <= MOD_ROWS
    assert cb % SCAN_SEQS == 0 and lb % SCAN_SEQS == 0
    depth = p["w_ada"].shape[0]
    cond = jnp.concatenate([c_ctx[None, :], c, jnp.zeros((MOD_ROWS - 1 - lb, d), F32)], axis=0)
    mod = ada_modulation(cond, p["w_ada"], p["b_ada"])
    xc = x_prompt.reshape(cb * ct, d)
    xl = x_sample.reshape(lb * lt, d)
    ctx_states = []
    for l in range(depth):
        w = _prepare_layer(p, l)
        xc, st = _layer(xc, (cb, ct), mod[l], w, l, None, grp=(0, cb * ct), tm=tm)
        xl, _ = _layer(xl, (lb, lt), mod[l], w, l, cache, grp=(1, lt), tm=tm)
        ctx_states.append(st)
    fw = final_norm_w.reshape(1, d)
    y_prompt = final_norm(xc, fw, tm=tm).reshape(cb, ct, d)
    y_sample = final_norm(xl, fw, tm=tm).reshape(lb, lt, d)

    def stack(name):
        return jnp.stack([s[name] for s in ctx_states], axis=1)

    return (y_prompt, y_sample, stack("attn_k"), stack("attn_v"), stack("mlstm_C"), stack("mlstm_n"),
            stack("mlstm_m"), stack("ret_S"), stack("gla_S"))


def kernel(x_prompt, x_sample, cache_attn_k, cache_attn_v, state_mlstm_C, state_mlstm_n, state_mlstm_m, state_ret_S, state_gla_S, c, c_ctx, w_ada, b_ada, norm1_w, norm2_w, w_in, attn_sink, mlstm_if_b, mlstm_norm_w, ret_decay, ret_norm_w, gla_w2, gla_b, gla_norm_w, w_br, w_mgate, w_out, ffn_w_gu, ffn_w_down, final_norm_w):
    p = {"w_ada": w_ada, "b_ada": b_ada, "norm1_w": norm1_w, "norm2_w": norm2_w, "w_in": w_in,
         "attn_sink": attn_sink, "mlstm_if_b": mlstm_if_b, "mlstm_norm_w": mlstm_norm_w,
         "ret_decay": ret_decay, "ret_norm_w": ret_norm_w, "gla_w2": gla_w2, "gla_b": gla_b,
         "gla_norm_w": gla_norm_w, "w_br": w_br, "w_mgate": w_mgate, "w_out": w_out,
         "ffn_w_gu": ffn_w_gu, "ffn_w_down": ffn_w_down}
    cache = {"attn_k": cache_attn_k, "attn_v": cache_attn_v, "mlstm_C": state_mlstm_C,
             "mlstm_n": state_mlstm_n, "mlstm_m": state_mlstm_m, "ret_S": state_ret_S, "gla_S": state_gla_S}
    return _forward(x_prompt, x_sample, cache, c, c_ctx, p, final_norm_w, tm=1024)
```
